```python
import math
import jax, jax.numpy as jnp
from jax import lax
import numpy as np

D_MODEL = 1024
BATCH = 2
SEQ = 8192
DEPTH = 4
DEC_BATCH = 128
DEC_SEQ = 8
PAST_LEN = 2048
PAGE_SIZE = 128

N_HEADS = 8
N_KV_HEADS = 4
GROUP = N_HEADS // N_KV_HEADS
HEAD_DIM = 64
V_DIM = 2 * HEAD_DIM
ATTN_SCALE = HEAD_DIM ** -0.5
Q_BLOCK = 128
CONV_WIDTH = D_MODEL
CONV_K = 3
FFN_DIM = ((-(-8 * D_MODEL // 3) + 255) // 256) * 256
N_BUCKETS = 32
MAX_EXACT = N_BUCKETS // 2
REL_MAX_DIST = 128
EPS = 1e-6
NEG_INF = -1e30
IN_SIZES = (N_HEADS * 2 * HEAD_DIM,
            N_KV_HEADS * 2 * HEAD_DIM,
            N_KV_HEADS * V_DIM,
            CONV_WIDTH,
            CONV_WIDTH,
            CONV_WIDTH,
            D_MODEL,
            D_MODEL)
IN_COLS = sum(IN_SIZES)
IN_OFFSETS = tuple(int(o) for o in np.cumsum(IN_SIZES)[:-1])

kernel_name = 'diff_attn_shortconv_hybrid_step'


def rmsnorm(x, g):
    xf = x.astype(jnp.float32)
    y = xf * lax.rsqrt(jnp.mean(xf * xf, axis=-1, keepdims=True) + EPS)
    return (y * g.astype(jnp.float32)).astype(x.dtype)


def rel_bucket(dist):
    n = jnp.maximum(dist, 0)
    nf = jnp.maximum(n, 1).astype(jnp.float32)
    large = MAX_EXACT + (jnp.log(nf / MAX_EXACT) / math.log(REL_MAX_DIST / MAX_EXACT)
                         * (N_BUCKETS - MAX_EXACT)).astype(jnp.int32)
    large = jnp.minimum(large, N_BUCKETS - 1)
    return jnp.where(n < MAX_EXACT, n, large)


def diff_attention(q, k, v, q_offset, lam, rel_bias):
    b, tq = q.shape[0], q.shape[1]
    tk = k.shape[1]
    blk = min(tq, Q_BLOCK)
    nb = tq // blk
    kpos = jnp.arange(tk, dtype=jnp.int32)
    q_blocks = jnp.swapaxes(q.reshape((b, nb, blk) + q.shape[2:]), 0, 1)
    starts = q_offset + blk * jnp.arange(nb, dtype=jnp.int32)

    def block(args):
        qb, start = args
        qpos = start + jnp.arange(blk, dtype=jnp.int32)
        dist = qpos[:, None] - kpos[None, :]
        bias = rel_bias[rel_bucket(dist)].astype(jnp.float32)
        bias = jnp.transpose(bias, (2, 0, 1)).reshape(N_KV_HEADS, GROUP, 1, blk, tk)
        s = jnp.einsum('bqhgmd,bkhmd->bhgmqk', qb, k,
                       preferred_element_type=jnp.float32) + bias
        s = jnp.where(dist >= 0, s, NEG_INF)
        p = jax.nn.softmax(s, axis=-1)
        a = p[:, :, :, 0] - lam * p[:, :, :, 1]
        o = jnp.einsum('bhgqk,bkhe->bqhge', a.astype(v.dtype), v,
                       preferred_element_type=jnp.float32)
        return o.reshape(b, blk, N_HEADS, V_DIM)

    out = lax.map(block, (q_blocks, starts))
    return jnp.swapaxes(out, 0, 1).reshape(b, tq, N_HEADS, V_DIM)


def mixer_sublayer(x, past_k, past_v, conv_state, q_offset, lam, lam_init,
                   mix_g, w_in, q_g, k_g, subln_g, conv_w, w_ao, w_co, w_o, rel_bias):
    b, t, _ = x.shape
    h = rmsnorm(x, mix_g)
    q, k, v, gate_b, gate_c, u, g_attn, g_conv = jnp.split(h @ w_in, IN_OFFSETS, axis=-1)
    q = rmsnorm(q.reshape(b, t, N_KV_HEADS, GROUP, 2, HEAD_DIM), q_g) * ATTN_SCALE
    k = rmsnorm(k.reshape(b, t, N_KV_HEADS, 2, HEAD_DIM), k_g)
    k_rows = k.reshape(b, t, N_KV_HEADS, 2 * HEAD_DIM)
    v_rows = v.reshape(b, t, N_KV_HEADS, V_DIM)
    if past_k is None:
        k_all, v_all = k_rows, v_rows
    else:
        k_all = jnp.concatenate([past_k.astype(k_rows.dtype), k_rows], axis=1)
        v_all = jnp.concatenate([past_v.astype(v_rows.dtype), v_rows], axis=1)
    attn = diff_attention(q, k_all.reshape(b, -1, N_KV_HEADS, 2, HEAD_DIM), v_all,
                          q_offset, lam, rel_bias)
    attn = (rmsnorm(attn, subln_g) * (1.0 - lam_init)).astype(x.dtype)
    attn = attn.reshape(b, t, N_HEADS * V_DIM)
    cu = gate_c * u
    cu_ext = jnp.concatenate([conv_state.astype(cu.dtype), cu], axis=1)
    conv = conv_w[0] * cu_ext[:, 0:t]
    for j in range(1, CONV_K):
        conv = conv + conv_w[j] * cu_ext[:, j:j + t]
    conv_y = gate_b * conv
    new_conv_state = cu_ext[:, t:]
    merged = (jax.nn.sigmoid(g_attn) * (attn @ w_ao)
              + jax.nn.sigmoid(g_conv) * (conv_y @ w_co))
    return x + merged @ w_o, k_rows, v_rows, new_conv_state


def swiglu_sublayer(x, norm_g, w_gate_up, w_down):
    h = rmsnorm(x, norm_g)
    gate, up = jnp.split(h @ w_gate_up, 2, axis=-1)
    return x + (jax.nn.silu(gate) * up) @ w_down


def setup_inputs(seed: int = 0) -> dict:
    key = jax.random.key(seed)
    ks = jax.random.split(key, 24)
    n_pages = PAST_LEN // PAGE_SIZE
    n_used = DEC_BATCH * n_pages
    n_pool = (n_used * 5 + 3) // 4

    def nrm(k, shape, scale):
        return jax.random.normal(k, shape, jnp.float32) * scale

    def gain(k, shape):
        return 1.0 + nrm(k, shape, 0.02)

    return {
        'x_prompt': nrm(ks[0], (BATCH, SEQ, D_MODEL), 1.0),
        'x_sample': nrm(ks[1], (DEC_BATCH, DEC_SEQ, D_MODEL), 1.0),
        'cache_k': nrm(ks[2], (DEPTH, n_pool, PAGE_SIZE, N_KV_HEADS, 2 * HEAD_DIM), 1.0),
        'cache_v': nrm(ks[3], (DEPTH, n_pool, PAGE_SIZE, N_KV_HEADS, V_DIM), 1.0),
        'state_conv': nrm(ks[4], (DEPTH, DEC_BATCH, CONV_K - 1, CONV_WIDTH), 1.0),
        'page_table': jax.random.permutation(ks[5], n_pool)[:n_used]
                      .reshape(DEC_BATCH, n_pages).astype(jnp.int32),
        'rel_bias': nrm(ks[6], (N_BUCKETS, N_HEADS), 0.2),
        'mix_norm_g': gain(ks[7], (DEPTH, D_MODEL)),
        'w_in': nrm(ks[8], (DEPTH, D_MODEL, IN_COLS), D_MODEL ** -0.5),
        'q_norm_g': gain(ks[9], (DEPTH, HEAD_DIM)),
        'k_norm_g': gain(ks[10], (DEPTH, HEAD_DIM)),
        'lambda_qk': nrm(ks[11], (DEPTH, 4, HEAD_DIM), 0.1),
        'subln_g': gain(ks[12], (DEPTH, V_DIM)),
        'conv_w': nrm(ks[13], (DEPTH, CONV_K, CONV_WIDTH), CONV_K ** -0.5),
        'w_attn_out': nrm(ks[14], (DEPTH, N_HEADS * V_DIM, D_MODEL), (N_HEADS * V_DIM) ** -0.5),
        'w_conv_out': nrm(ks[15], (DEPTH, CONV_WIDTH, D_MODEL), CONV_WIDTH ** -0.5),
        'w_out': nrm(ks[16], (DEPTH, D_MODEL, D_MODEL), D_MODEL ** -0.5),
        'ffn_norm_g': gain(ks[17], (DEPTH, D_MODEL)),
        'w_gate_up': nrm(ks[18], (DEPTH, D_MODEL, 2 * FFN_DIM), D_MODEL ** -0.5),
        'w_down': nrm(ks[19], (DEPTH, FFN_DIM, D_MODEL), FFN_DIM ** -0.5),
    }


def reference(x_prompt, x_sample, cache_k, cache_v, state_conv, page_table, rel_bias,
              mix_norm_g, w_in, q_norm_g, k_norm_g, lambda_qk, subln_g, conv_w,
              w_attn_out, w_conv_out, w_out, ffn_norm_g, w_gate_up, w_down):
    yp, ys = x_prompt, x_sample
    zero_conv = jnp.zeros((x_prompt.shape[0], CONV_K - 1, CONV_WIDTH), x_prompt.dtype)
    kp_l, vp_l, cp_l, ks_l, vs_l, cs_l = [], [], [], [], [], []
    for l in range(DEPTH):
        lam_init = 0.8 - 0.6 * math.exp(-0.3 * l)
        lq = lambda_qk[l].astype(jnp.float32)
        lam = jnp.exp(jnp.sum(lq[0] * lq[1])) - jnp.exp(jnp.sum(lq[2] * lq[3])) + lam_init
        lw = (mix_norm_g[l], w_in[l], q_norm_g[l], k_norm_g[l], subln_g[l], conv_w[l],
              w_attn_out[l], w_conv_out[l], w_out[l], rel_bias)
        yp, kp, vp, cp = mixer_sublayer(yp, None, None, zero_conv, 0, lam, lam_init, *lw)
        pk = cache_k[l, page_table]
        pv = cache_v[l, page_table]
        pk = pk.reshape(pk.shape[0], -1, N_KV_HEADS, 2 * HEAD_DIM)
        pv = pv.reshape(pv.shape[0], -1, N_KV_HEADS, V_DIM)
        ys, ksn, vsn, csn = mixer_sublayer(ys, pk, pv, state_conv[l], pk.shape[1],
                                           lam, lam_init, *lw)
        yp = swiglu_sublayer(yp, ffn_norm_g[l], w_gate_up[l], w_down[l])
        ys = swiglu_sublayer(ys, ffn_norm_g[l], w_gate_up[l], w_down[l])
        kp_l.append(kp); vp_l.append(vp); cp_l.append(cp)
        ks_l.append(ksn); vs_l.append(vsn); cs_l.append(csn)
    new_k_prompt = jnp.stack(kp_l)
    new_v_prompt = jnp.stack(vp_l)
    new_conv_prompt = jnp.stack(cp_l)
    new_k_sample = jnp.stack(ks_l)
    new_v_sample = jnp.stack(vs_l)
    new_conv_sample = jnp.stack(cs_l)
    return (yp, ys, new_k_prompt, new_v_prompt, new_conv_prompt,
            new_k_sample, new_v_sample, new_conv_sample)
```

```python
import functools
import math

import jax
import jax.numpy as jnp
import numpy as np
from jax import lax
from jax.experimental import pallas as pl
from jax.experimental.pallas import tpu as pltpu

F32 = jnp.float32
BF16 = jnp.bfloat16

D_MODEL = 1024
N_HEADS = 8
N_KV_HEADS = 4
GROUP = N_HEADS // N_KV_HEADS
HEAD_DIM = 64
V_DIM = 2 * HEAD_DIM
ATTN_SCALE = HEAD_DIM ** -0.5
CONV_K = 3
FFN_DIM = 2816
N_BUCKETS = 32
MAX_EXACT = N_BUCKETS // 2
REL_MAX_DIST = 128
EPS = 1e-6
NEG_INF = -1e30
PAGE_SIZE = 128

OFF_Q, OFF_K, OFF_V, OFF_GB, OFF_GC, OFF_U, OFF_GA, OFF_GCV, IN_COLS = (
    0, 1024, 1536, 2048, 3072, 4096, 5120, 6144, 7168)

VMEM_LIMIT_BYTES = 56 * 1024 * 1024
SUBLANES = 8
ROW_TILE = 512
ATTN_TILE = 256
FFN_CHUNKS = ((0, 1536), (1536, 2816))


def _params(n_axes, vmem=VMEM_LIMIT_BYTES):
    return pltpu.CompilerParams(
        dimension_semantics=("arbitrary",) * n_axes, vmem_limit_bytes=vmem)


def _const_spec(shape):
    zeros = (0,) * len(shape)
    return pl.BlockSpec(shape, lambda *_: zeros, pipeline_mode=pl.Buffered(1))


def _rel_buckets(max_dist):
    n = np.arange(max_dist + 1)
    nf = np.maximum(n, 1).astype(np.float32)
    large = MAX_EXACT + (np.log(nf / np.float32(MAX_EXACT)) / np.float32(math.log(REL_MAX_DIST / MAX_EXACT))
                         * np.float32(N_BUCKETS - MAX_EXACT)).astype(np.int32)
    large = np.minimum(large, N_BUCKETS - 1)
    return np.where(n < MAX_EXACT, n, large).astype(np.int32)


def _inproj_kernel(*refs, tm, tiles_per_seq, sample):
    if sample:
        (x_ref, g_ref, w_ref, bd_ref, qg_ref, kg_ref, cw_ref, s1_ref, s2_ref,
         q_ref, k_ref, v_ref, cy_ref, sga_ref, sgc_ref, cu_ref, ext) = refs
    else:
        (x_ref, g_ref, w_ref, bd_ref, qg_ref, kg_ref, cw_ref,
         q_ref, k_ref, v_ref, cy_ref, sga_ref, sgc_ref, cu_ref, ext) = refs
    i = pl.program_id(0)

    x = x_ref[...]
    ms = jnp.mean(x * x, axis=-1, keepdims=True)
    h = (x * lax.rsqrt(ms + EPS) * g_ref[...]).astype(BF16)

    def proj(a, b):
        return jnp.dot(h, w_ref[:, a:b], preferred_element_type=F32)

    def group_norm(y, bd, gain):
        ss = jnp.dot((y * y).astype(BF16), bd, preferred_element_type=F32)
        return y * lax.rsqrt(ss * (1.0 / HEAD_DIM) + EPS) * gain

    yq = proj(OFF_Q, OFF_K)
    q_ref[...] = (group_norm(yq, bd_ref[...], qg_ref[...]) * ATTN_SCALE).astype(q_ref.dtype)
    yk = proj(OFF_K, OFF_V)
    k_ref[...] = group_norm(yk, bd_ref[0:512, 0:512], kg_ref[...])
    v_ref[...] = proj(OFF_V, OFF_GB)

    @pl.when(i % tiles_per_seq == 0)
    def _():
        ext[0:SUBLANES, :] = jnp.zeros((SUBLANES, D_MODEL), F32)

    cu = proj(OFF_GC, OFF_U) * proj(OFF_U, OFF_GA)
    ext[SUBLANES:SUBLANES + tm, :] = cu
    p1 = ext[SUBLANES - 1:SUBLANES - 1 + tm, :]
    p2 = ext[SUBLANES - 2:SUBLANES - 2 + tm, :]
    if sample:
        rmod = lax.broadcasted_iota(jnp.int32, (tm, D_MODEL), 0) & (SUBLANES - 1)
        p1 = jnp.where(rmod >= 1, p1, s1_ref[...])
        p2 = jnp.where(rmod >= 2, p2, s2_ref[...])
    cw = cw_ref[...]
    conv = cw[0:1, :] * p2 + cw[1:2, :] * p1 + cw[2:3, :] * cu
    cy_ref[...] = (proj(OFF_GB, OFF_GC) * conv).astype(BF16)
    if sample:
        cu_ref[...] = cu
    else:
        cu_ref[...] = cu[tm - SUBLANES:tm, :]
        ext[0:SUBLANES, :] = cu[tm - SUBLANES:tm, :]

    sga_ref[...] = jax.nn.sigmoid(proj(OFF_GA, OFF_GCV)).astype(BF16)
    sgc_ref[...] = jax.nn.sigmoid(proj(OFF_GCV, IN_COLS)).astype(BF16)


def _inproj(x, g, w_in, bd, qg, kg, cw, *, seq_len, state=None):
    t = x.shape[0]
    tm = min(ROW_TILE, t)
    n_tiles = t // tm
    sample = state is not None
    tiles_per_seq = max(seq_len // tm, 1)
    row = lambda c: pl.BlockSpec((tm, c), lambda i: (i, 0))
    in_specs = [row(D_MODEL), _const_spec((1, D_MODEL)), _const_spec((D_MODEL, IN_COLS)),
                _const_spec((D_MODEL, D_MODEL)), _const_spec((1, D_MODEL)),
                _const_spec((1, 512)), _const_spec((CONV_K, D_MODEL))]
    args = [x, g, w_in, bd, qg, kg, cw]
    if sample:
        in_specs += [row(D_MODEL), row(D_MODEL)]
        args += list(state)
        cu_shape, cu_spec = (t, D_MODEL), row(D_MODEL)
    else:
        cu_shape = (n_tiles * SUBLANES, D_MODEL)
        cu_spec = pl.BlockSpec((SUBLANES, D_MODEL), lambda i: (i, 0))
    out_shape = [jax.ShapeDtypeStruct((t, D_MODEL), F32 if sample else BF16),
                 jax.ShapeDtypeStruct((t, 512), F32),
                 jax.ShapeDtypeStruct((t, 512), F32),
                 jax.ShapeDtypeStruct((t, D_MODEL), BF16),
                 jax.ShapeDtypeStruct((t, D_MODEL), BF16),
                 jax.ShapeDtypeStruct((t, D_MODEL), BF16),
                 jax.ShapeDtypeStruct(cu_shape, F32)]
    out_specs = [row(D_MODEL), row(512), row(512), row(D_MODEL), row(D_MODEL),
                 row(D_MODEL), cu_spec]
    return pl.pallas_call(
        functools.partial(_inproj_kernel, tm=tm, tiles_per_seq=tiles_per_seq, sample=sample),
        grid=(n_tiles,), in_specs=in_specs, out_specs=out_specs, out_shape=out_shape,
        scratch_shapes=[pltpu.VMEM((tm + SUBLANES, D_MODEL), F32)],
        compiler_params=_params(1),
        name="inproj_sample" if sample else "inproj_prompt",
    )(*args)


def _lambda(lq_ref, lam_init):
    lq = lq_ref[...]
    a = jnp.sum(lq[0:1, :] * lq[1:2, :], axis=-1, keepdims=True)
    b = jnp.sum(lq[2:3, :] * lq[3:4, :], axis=-1, keepdims=True)
    return jnp.exp(a) - jnp.exp(b) + lam_init


def _head_norm(o, gain, lam_init):
    ms = jnp.mean(o * o, axis=-1, keepdims=True)
    return o * lax.rsqrt(ms + EPS) * gain * (1.0 - lam_init)


def _attn_prompt_kernel(q_ref, k_ref, v_ref, bdiag_ref, bsub_ref, lq_ref, sg_ref, o_ref,
                        k1_s, k2_s, v_s, qs_s, m_s, l_s, acc_s, *, tb, seq_len, lam_init):
    qi = pl.program_id(2)
    rows2 = GROUP * tb
    cast_rows = 512

    @pl.when(qi == 0)
    def _():
        lane = lax.broadcasted_iota(jnp.int32, (cast_rows, V_DIM), 1)

        def body(c, carry):
            rows = pl.ds(pl.multiple_of(c * cast_rows, cast_rows), cast_rows)
            kf = k_ref[0, rows, :]
            k1_s[rows, :] = jnp.where(lane < HEAD_DIM, kf, 0.0).astype(BF16)
            k2_s[rows, :] = jnp.where(lane >= HEAD_DIM, kf, 0.0).astype(BF16)
            v_s[rows, :] = v_ref[0, rows, :].astype(BF16)
            return carry

        lax.fori_loop(0, seq_len // cast_rows, body, 0)

    for g in range(GROUP):
        qs_s[g * tb:(g + 1) * tb, :] = q_ref[0, :, g * V_DIM:(g + 1) * V_DIM]
    m_s[...] = jnp.full(m_s.shape, NEG_INF, F32)
    l_s[...] = jnp.zeros(l_s.shape, F32)
    acc_s[...] = jnp.zeros(acc_s.shape, F32)

    def block(kj, bias):
        rows = pl.ds(pl.multiple_of(kj * tb, tb), tb)
        q = qs_s[...]
        vb = v_s[rows, :]
        for mi, ks in enumerate((k1_s, k2_s)):
            s = lax.dot_general(q, ks[rows, :], (((1,), (1,)), ((), ())),
                                preferred_element_type=F32)
            if bias is not None:
                s = s + bias
            m_old = m_s[mi]
            m_new = jnp.maximum(m_old, jnp.max(s, axis=-1, keepdims=True))
            alpha = jnp.exp(m_old - m_new)
            p = jnp.exp(s - m_new)
            l_s[mi] = alpha * l_s[mi] + jnp.sum(p, axis=-1, keepdims=True)
            acc_s[mi] = alpha * acc_s[mi] + jnp.dot(p.astype(BF16), vb,
                                                    preferred_element_type=F32)
            m_s[mi] = m_new

    def far(kj, carry):
        block(kj, None)
        return carry

    lax.fori_loop(0, qi - 1, far, 0)

    @pl.when(qi >= 1)
    def _():
        block(qi - 1, bsub_ref[0])

    block(qi, bdiag_ref[0])

    lam = _lambda(lq_ref, lam_init)
    o = acc_s[0] / l_s[0] - lam * (acc_s[1] / l_s[1])
    o = _head_norm(o, sg_ref[...], lam_init)
    for g in range(GROUP):
        o_ref[0, :, g * V_DIM:(g + 1) * V_DIM] = o[g * tb:(g + 1) * tb, :].astype(o_ref.dtype)


def _attn_prompt(q, k, v, bdiag, bsub, lq, sg, *, lam_init):
    b, s, _ = q.shape
    tb = min(ATTN_TILE, s)
    rows2 = GROUP * tb
    kv_spec = pl.BlockSpec((1, s, V_DIM), lambda bi, h, qi: (bi, 0, h))
    q_spec = pl.BlockSpec((1, tb, GROUP * V_DIM), lambda bi, h, qi: (bi, qi, h))
    bias_spec = pl.BlockSpec((1, rows2, tb), lambda bi, h, qi: (h, 0, 0))
    return pl.pallas_call(
        functools.partial(_attn_prompt_kernel, tb=tb, seq_len=s, lam_init=lam_init),
        grid=(b, N_KV_HEADS, s // tb),
        in_specs=[q_spec, kv_spec, kv_spec, bias_spec, bias_spec,
                  _const_spec((4, HEAD_DIM)), _const_spec((1, V_DIM))],
        out_specs=q_spec,
        out_shape=jax.ShapeDtypeStruct((b, s, D_MODEL), BF16),
        scratch_shapes=[pltpu.VMEM((s, V_DIM), BF16), pltpu.VMEM((s, V_DIM), BF16),
                        pltpu.VMEM((s, V_DIM), BF16), pltpu.VMEM((rows2, V_DIM), BF16),
                        pltpu.VMEM((2, rows2, 1), F32), pltpu.VMEM((2, rows2, 1), F32),
                        pltpu.VMEM((2, rows2, V_DIM), F32)],
        compiler_params=_params(3),
        name="attn_prompt",
    )(q, k, v, bdiag, bsub, lq, sg)


def _prompt_bias(rel_bias, tb):
    buckets = _rel_buckets(2 * tb)
    r = np.arange(tb)[:, None]
    c = np.arange(tb)[None, :]
    tbl = rel_bias - rel_bias[N_BUCKETS - 1:N_BUCKETS, :]

    def tile(delta):
        dist = delta + r - c
        vals = tbl[buckets[np.maximum(dist, 0)], :]
        vals = jnp.where((dist >= 0)[:, :, None], vals, NEG_INF)
        vals = jnp.transpose(vals, (2, 0, 1))
        return vals.reshape(N_KV_HEADS, GROUP * tb, tb)

    return tile(0), tile(tb)


def _attn_sample_kernel(pt_ref, q_ref, kn_ref, vn_ref, bias_ref, lq_ref, sg_ref, *rest,
                        n_pages, t_new, lam_init):
    k_refs = rest[:n_pages]
    v_refs = rest[n_pages:2 * n_pages]
    o_ref, kb_s, vb_s = rest[2 * n_pages:]
    past = n_pages * PAGE_SIZE
    width = N_KV_HEADS * V_DIM

    for j in range(n_pages):
        kb_s[j * PAGE_SIZE:(j + 1) * PAGE_SIZE, :] = k_refs[j][0, 0].astype(BF16)
        vb_s[j * PAGE_SIZE:(j + 1) * PAGE_SIZE, :] = v_refs[j][0, 0].astype(BF16)
    pad = jnp.zeros((PAGE_SIZE - t_new, width), F32)
    kb_s[past:past + PAGE_SIZE, :] = jnp.concatenate([kn_ref[0], pad], axis=0).astype(BF16)
    vb_s[past:past + PAGE_SIZE, :] = jnp.concatenate([vn_ref[0], pad], axis=0).astype(BF16)

    qf = q_ref[0]
    lane = lax.broadcasted_iota(jnp.int32, (t_new, V_DIM), 1)
    zero = jnp.zeros((t_new, V_DIM), F32)
    groups = []
    for mi in range(2):
        keep = (lane < HEAD_DIM) if mi == 0 else (lane >= HEAD_DIM)
        for h in range(N_KV_HEADS):
            for g in range(GROUP):
                c0 = (h * GROUP + g) * V_DIM
                blk = jnp.where(keep, qf[:, c0:c0 + V_DIM], 0.0)
                groups.append(jnp.concatenate(
                    [blk if hh == h else zero for hh in range(N_KV_HEADS)], axis=1))
    wq = jnp.concatenate(groups, axis=0).astype(BF16)

    s = lax.dot_general(wq, kb_s[...], (((1,), (1,)), ((), ())),
                        preferred_element_type=F32) + bias_ref[...]
    m = jnp.max(s, axis=-1, keepdims=True)
    p = jnp.exp(s - m)
    l = jnp.sum(p, axis=-1, keepdims=True)
    acc = jnp.dot(p.astype(BF16), vb_s[...], preferred_element_type=F32) / l

    lam = _lambda(lq_ref, lam_init)
    half = N_HEADS * t_new
    for h in range(N_KV_HEADS):
        for g in range(GROUP):
            r0 = (h * GROUP + g) * t_new
            o1 = acc[r0:r0 + t_new, h * V_DIM:(h + 1) * V_DIM]
            o2 = acc[half + r0:half + r0 + t_new, h * V_DIM:(h + 1) * V_DIM]
            o = _head_norm(o1 - lam * o2, sg_ref[...], lam_init)
            c0 = (h * GROUP + g) * V_DIM
            o_ref[0, :, c0:c0 + V_DIM] = o


def _attn_sample(page_table, q, kn, vn, bias, lq, sg, cache_k, cache_v, *, layer, lam_init):
    n, t_new, _ = q.shape
    n_pages = page_table.shape[1]
    width = N_KV_HEADS * V_DIM
    cols = (n_pages + 1) * PAGE_SIZE
    seq = lambda c: pl.BlockSpec((1, t_new, c), lambda i, pt: (i, 0, 0))
    const = lambda shape: pl.BlockSpec(shape, lambda i, pt: (0,) * len(shape),
                                       pipeline_mode=pl.Buffered(1))

    def page_spec(j):
        return pl.BlockSpec((1, 1, PAGE_SIZE, width),
                            lambda i, pt: (layer, pt[i * n_pages + j], 0, 0))

    pages = [page_spec(j) for j in range(n_pages)]
    grid_spec = pltpu.PrefetchScalarGridSpec(
        num_scalar_prefetch=1, grid=(n,),
        in_specs=[seq(D_MODEL), seq(width), seq(width), const((2 * N_HEADS * t_new, cols)),
                  const((4, HEAD_DIM)), const((1, V_DIM))] + pages + pages,
        out_specs=seq(D_MODEL),
        scratch_shapes=[pltpu.VMEM((cols, width), BF16), pltpu.VMEM((cols, width), BF16)])
    return pl.pallas_call(
        functools.partial(_attn_sample_kernel, n_pages=n_pages, t_new=t_new, lam_init=lam_init),
        grid_spec=grid_spec,
        out_shape=jax.ShapeDtypeStruct((n, t_new, D_MODEL), F32),
        compiler_params=_params(1),
        name="attn_sample",
    )(page_table.reshape(-1), q, kn, vn, bias, lq, sg,
      *([cache_k] * n_pages), *([cache_v] * n_pages))


def _sample_bias(rel_bias, n_pages, t_new):
    past = n_pages * PAGE_SIZE
    cols = past + PAGE_SIZE
    buckets = _rel_buckets(past + t_new)
    qpos = past + np.arange(t_new)[:, None]
    kpos = np.arange(cols)[None, :]
    dist = qpos - kpos
    valid = (kpos < past + t_new) & (dist >= 0)
    tbl = rel_bias - rel_bias[N_BUCKETS - 1:N_BUCKETS, :]
    vals = tbl[buckets[np.clip(dist, 0, past + t_new)], :]
    vals = jnp.where(valid[:, :, None], vals, NEG_INF)
    vals = jnp.transpose(vals, (2, 0, 1)).reshape(N_HEADS * t_new, cols)
    return jnp.concatenate([vals, vals], axis=0)


def _mixout_kernel(a_ref, cy_ref, sga_ref, sgc_ref, x_ref, wao_ref, wco_ref, wo_ref, o_ref):
    a = jnp.dot(a_ref[...].astype(BF16), wao_ref[...], preferred_element_type=F32)
    c = jnp.dot(cy_ref[...], wco_ref[...], preferred_element_type=F32)
    merged = sga_ref[...].astype(F32) * a + sgc_ref[...].astype(F32) * c
    o_ref[...] = x_ref[...] + jnp.dot(merged.astype(BF16), wo_ref[...],
                                      preferred_element_type=F32)


def _mixout(attn, cy, sga, sgc, x, wao, wco, wo):
    t = x.shape[0]
    tm = min(ROW_TILE, t)
    row = pl.BlockSpec((tm, D_MODEL), lambda i: (i, 0))
    w = _const_spec((D_MODEL, D_MODEL))
    return pl.pallas_call(
        _mixout_kernel, grid=(t // tm,),
        in_specs=[row, row, row, row, row, w, w, w], out_specs=row,
        out_shape=jax.ShapeDtypeStruct((t, D_MODEL), F32),
        compiler_params=_params(1), name="mixout",
    )(attn, cy, sga, sgc, x, wao, wco, wo)


def _ffn_kernel(x_ref, g_ref, wgu_ref, wd_ref, o_ref):
    x = x_ref[...]
    ms = jnp.mean(x * x, axis=-1, keepdims=True)
    h = (x * lax.rsqrt(ms + EPS) * g_ref[...]).astype(BF16)
    y = None
    for a, b in FFN_CHUNKS:
        gate = jnp.dot(h, wgu_ref[:, a:b], preferred_element_type=F32)
        up = jnp.dot(h, wgu_ref[:, FFN_DIM + a:FFN_DIM + b], preferred_element_type=F32)
        act = (gate * jax.nn.sigmoid(gate) * up).astype(BF16)
        part = jnp.dot(act, wd_ref[a:b, :], preferred_element_type=F32)
        y = part if y is None else y + part
    o_ref[...] = x + y


def _ffn(x, g, wgu, wd):
    t = x.shape[0]
    tm = min(ROW_TILE, t)
    row = pl.BlockSpec((tm, D_MODEL), lambda i: (i, 0))
    return pl.pallas_call(
        _ffn_kernel, grid=(t // tm,),
        in_specs=[row, _const_spec((1, D_MODEL)), _const_spec((D_MODEL, 2 * FFN_DIM)),
                  _const_spec((FFN_DIM, D_MODEL))],
        out_specs=row, out_shape=jax.ShapeDtypeStruct((t, D_MODEL), F32),
        compiler_params=_params(1), name="ffn",
    )(x, g, wgu, wd)


def kernel(x_prompt, x_sample, cache_k, cache_v, state_conv, page_table, rel_bias, mix_norm_g, w_in, q_norm_g, k_norm_g, lambda_qk, subln_g, conv_w, w_attn_out, w_conv_out, w_out, ffn_norm_g, w_gate_up, w_down):
    batch, seq, _ = x_prompt.shape
    n_dec, t_new, _ = x_sample.shape
    depth = w_in.shape[0]
    n_pool = cache_k.shape[1]
    n_pages = page_table.shape[1]
    width = N_KV_HEADS * V_DIM

    w_in_b, w_ao_b, w_co_b, w_o_b, w_gu_b, w_d_b = (
        w.astype(BF16) for w in (w_in, w_attn_out, w_conv_out, w_out, w_gate_up, w_down))
    blk = np.arange(D_MODEL) // HEAD_DIM
    bd = jnp.asarray(blk[:, None] == blk[None, :], BF16)
    cache_k4 = cache_k.reshape(depth, n_pool, PAGE_SIZE, width)
    cache_v4 = cache_v.reshape(depth, n_pool, PAGE_SIZE, width)

    tb = min(ATTN_TILE, seq)
    bdiag, bsub = _prompt_bias(rel_bias, tb)
    sbias = _sample_bias(rel_bias, n_pages, t_new)

    first = jnp.zeros((depth, n_dec, t_new - 1, D_MODEL), F32)
    s1 = jnp.concatenate([state_conv[:, :, 1:2], first], axis=2)
    s2 = jnp.concatenate([state_conv, first[:, :, 1:]], axis=2)
    s1 = s1.reshape(depth, n_dec * t_new, D_MODEL)
    s2 = s2.reshape(depth, n_dec * t_new, D_MODEL)

    xp = x_prompt.reshape(batch * seq, D_MODEL)
    xs = x_sample.reshape(n_dec * t_new, D_MODEL)
    tiles_per_seq = max(seq // min(ROW_TILE, batch * seq), 1)
    outs = [[] for _ in range(6)]
    for l in range(depth):
        lam_init = 0.8 - 0.6 * math.exp(-0.3 * l)
        g_mix = mix_norm_g[l].reshape(1, D_MODEL)
        qg = jnp.tile(q_norm_g[l], D_MODEL // HEAD_DIM).reshape(1, D_MODEL)
        kg = jnp.tile(k_norm_g[l], width // HEAD_DIM).reshape(1, width)
        sg = subln_g[l].reshape(1, V_DIM)
        lq = lambda_qk[l]

        q, k, v, cy, sga, sgc, cu = _inproj(xp, g_mix, w_in_b[l], bd, qg, kg, conv_w[l],
                                            seq_len=seq)
        attn = _attn_prompt(q.reshape(batch, seq, D_MODEL), k.reshape(batch, seq, width),
                            v.reshape(batch, seq, width), bdiag, bsub, lq, sg,
                            lam_init=lam_init)
        xp = _mixout(attn.reshape(batch * seq, D_MODEL), cy, sga, sgc, xp,
                     w_ao_b[l], w_co_b[l], w_o_b[l])
        xp = _ffn(xp, ffn_norm_g[l].reshape(1, D_MODEL), w_gu_b[l], w_d_b[l])
        outs[0].append(k.reshape(batch, seq, N_KV_HEADS, V_DIM))
        outs[1].append(v.reshape(batch, seq, N_KV_HEADS, V_DIM))
        tails = cu.reshape(batch, tiles_per_seq, SUBLANES, D_MODEL)
        outs[2].append(tails[:, -1, SUBLANES - (CONV_K - 1):, :])

        q, k, v, cy, sga, sgc, cu = _inproj(xs, g_mix, w_in_b[l], bd, qg, kg, conv_w[l],
                                            seq_len=t_new, state=(s1[l], s2[l]))
        attn = _attn_sample(page_table, q.reshape(n_dec, t_new, D_MODEL),
                            k.reshape(n_dec, t_new, width), v.reshape(n_dec, t_new, width),
                            sbias, lq, sg, cache_k4, cache_v4, layer=l, lam_init=lam_init)
        xs = _mixout(attn.reshape(n_dec * t_new, D_MODEL), cy, sga, sgc, xs,
                     w_ao_b[l], w_co_b[l], w_o_b[l])
        xs = _ffn(xs, ffn_norm_g[l].reshape(1, D_MODEL), w_gu_b[l], w_d_b[l])
        outs[3].append(k.reshape(n_dec, t_new, N_KV_HEADS, V_DIM))
        outs[4].append(v.reshape(n_dec, t_new, N_KV_HEADS, V_DIM))
        outs[5].append(cu.reshape(n_dec, t_new, D_MODEL)[:, t_new - (CONV_K - 1):, :])

    return (xp.reshape(batch, seq, D_MODEL), xs.reshape(n_dec, t_new, D_MODEL),
            jnp.stack(outs[0]), jnp.stack(outs[1]), jnp.stack(outs[2]),
            jnp.stack(outs[3]), jnp.stack(outs[4]), jnp.stack(outs[5]))
```

```python
import functools
import math

import jax
import jax.numpy as jnp
import numpy as np
from jax import lax
from jax.experimental import pallas as pl
from jax.experimental.pallas import tpu as pltpu

F32 = jnp.float32
BF16 = jnp.bfloat16

D_MODEL = 1024
N_HEADS = 8
N_KV_HEADS = 4
GROUP = N_HEADS // N_KV_HEADS
HEAD_DIM = 64
V_DIM = 2 * HEAD_DIM
ATTN_SCALE = HEAD_DIM ** -0.5
CONV_K = 3
FFN_DIM = 2816
N_BUCKETS = 32
MAX_EXACT = N_BUCKETS // 2
REL_MAX_DIST = 128
EPS = 1e-6
NEG_INF = -1e30
PAGE_SIZE = 128

OFF_Q, OFF_K, OFF_V, OFF_GB, OFF_GC, OFF_U, OFF_GA, OFF_GCV, IN_COLS = (
    0, 1024, 1536, 2048, 3072, 4096, 5120, 6144, 7168)

VMEM_LIMIT_BYTES = 56 * 1024 * 1024
SUBLANES = 8
LANES = 128
LOG2E = math.log2(math.e)
ROW_TILE = 512
ATTN_TILE = 512
FFN_CHUNKS = ((0, 1536), (1536, 2816))


def _params(n_axes, vmem=VMEM_LIMIT_BYTES):
    return pltpu.CompilerParams(
        dimension_semantics=("arbitrary",) * n_axes, vmem_limit_bytes=vmem)


def _const_spec(shape):
    zeros = (0,) * len(shape)
    return pl.BlockSpec(shape, lambda *_: zeros, pipeline_mode=pl.Buffered(1))


def _rel_buckets(max_dist):
    n = np.arange(max_dist + 1)
    nf = np.maximum(n, 1).astype(np.float32)
    large = MAX_EXACT + (np.log(nf / np.float32(MAX_EXACT)) / np.float32(math.log(REL_MAX_DIST / MAX_EXACT))
                         * np.float32(N_BUCKETS - MAX_EXACT)).astype(np.int32)
    large = np.minimum(large, N_BUCKETS - 1)
    return np.where(n < MAX_EXACT, n, large).astype(np.int32)


def _inproj_kernel(*refs, tm, tiles_per_seq, sample):
    if sample:
        (x_ref, g_ref, w_ref, bd_ref, qg_ref, kg_ref, cw_ref, s1_ref, s2_ref,
         q_ref, k_ref, v_ref, cy_ref, sga_ref, sgc_ref, cu_ref, ext) = refs
    else:
        (x_ref, g_ref, w_ref, bd_ref, qg_ref, kg_ref, cw_ref,
         q_ref, k_ref, v_ref, cy_ref, sga_ref, sgc_ref, cu_ref, ext) = refs
    i = pl.program_id(0)

    x = x_ref[...]
    ms = jnp.mean(x * x, axis=-1, keepdims=True)
    h = (x * lax.rsqrt(ms + EPS) * g_ref[...]).astype(BF16)

    def proj(a, b):
        return jnp.dot(h, w_ref[:, a:b], preferred_element_type=F32)

    def group_norm(y, bd, gain):
        ss = jnp.dot((y * y).astype(BF16), bd, preferred_element_type=F32)
        return y * lax.rsqrt(ss * (1.0 / HEAD_DIM) + EPS) * gain

    yq = proj(OFF_Q, OFF_K)
    q_ref[...] = (group_norm(yq, bd_ref[...], qg_ref[...]) * (ATTN_SCALE * LOG2E)).astype(q_ref.dtype)
    yk = proj(OFF_K, OFF_V)
    k_ref[...] = group_norm(yk, bd_ref[0:512, 0:512], kg_ref[...])
    v_ref[...] = proj(OFF_V, OFF_GB)

    @pl.when(i % tiles_per_seq == 0)
    def _():
        ext[0:SUBLANES, :] = jnp.zeros((SUBLANES, D_MODEL), F32)

    cu = proj(OFF_GC, OFF_U) * proj(OFF_U, OFF_GA)
    ext[SUBLANES:SUBLANES + tm, :] = cu
    p1 = ext[SUBLANES - 1:SUBLANES - 1 + tm, :]
    p2 = ext[SUBLANES - 2:SUBLANES - 2 + tm, :]
    if sample:
        rmod = lax.broadcasted_iota(jnp.int32, (tm, D_MODEL), 0) & (SUBLANES - 1)
        p1 = jnp.where(rmod >= 1, p1, s1_ref[...])
        p2 = jnp.where(rmod >= 2, p2, s2_ref[...])
    cw = cw_ref[...]
    conv = cw[0:1, :] * p2 + cw[1:2, :] * p1 + cw[2:3, :] * cu
    cy_ref[...] = (proj(OFF_GB, OFF_GC) * conv).astype(BF16)
    if sample:
        cu_ref[...] = cu
    else:
        cu_ref[...] = cu[tm - SUBLANES:tm, :]
        ext[0:SUBLANES, :] = cu[tm - SUBLANES:tm, :]

    sga_ref[...] = jax.nn.sigmoid(proj(OFF_GA, OFF_GCV)).astype(BF16)
    sgc_ref[...] = jax.nn.sigmoid(proj(OFF_GCV, IN_COLS)).astype(BF16)


def _inproj(x, g, w_in, bd, qg, kg, cw, *, seq_len, state=None):
    t = x.shape[0]
    tm = min(ROW_TILE, t)
    n_tiles = t // tm
    sample = state is not None
    tiles_per_seq = max(seq_len // tm, 1)
    row = lambda c: pl.BlockSpec((tm, c), lambda i: (i, 0))
    in_specs = [row(D_MODEL), _const_spec((1, D_MODEL)), _const_spec((D_MODEL, IN_COLS)),
                _const_spec((D_MODEL, D_MODEL)), _const_spec((1, D_MODEL)),
                _const_spec((1, 512)), _const_spec((CONV_K, D_MODEL))]
    args = [x, g, w_in, bd, qg, kg, cw]
    if sample:
        in_specs += [row(D_MODEL), row(D_MODEL)]
        args += list(state)
        cu_shape, cu_spec = (t, D_MODEL), row(D_MODEL)
    else:
        cu_shape = (n_tiles * SUBLANES, D_MODEL)
        cu_spec = pl.BlockSpec((SUBLANES, D_MODEL), lambda i: (i, 0))
    out_shape = [jax.ShapeDtypeStruct((t, D_MODEL), F32 if sample else BF16),
                 jax.ShapeDtypeStruct((t, 512), F32),
                 jax.ShapeDtypeStruct((t, 512), F32),
                 jax.ShapeDtypeStruct((t, D_MODEL), BF16),
                 jax.ShapeDtypeStruct((t, D_MODEL), BF16),
                 jax.ShapeDtypeStruct((t, D_MODEL), BF16),
                 jax.ShapeDtypeStruct(cu_shape, F32)]
    out_specs = [row(D_MODEL), row(512), row(512), row(D_MODEL), row(D_MODEL),
                 row(D_MODEL), cu_spec]
    return pl.pallas_call(
        functools.partial(_inproj_kernel, tm=tm, tiles_per_seq=tiles_per_seq, sample=sample),
        grid=(n_tiles,), in_specs=in_specs, out_specs=out_specs, out_shape=out_shape,
        scratch_shapes=[pltpu.VMEM((tm + SUBLANES, D_MODEL), F32)],
        compiler_params=_params(1),
        name="inproj_sample" if sample else "inproj_prompt",
    )(*args)


def _lambda(lq_ref, lam_init):
    lq = lq_ref[...]
    a = jnp.sum(lq[0:1, :] * lq[1:2, :], axis=-1, keepdims=True)
    b = jnp.sum(lq[2:3, :] * lq[3:4, :], axis=-1, keepdims=True)
    return jnp.exp(a) - jnp.exp(b) + lam_init


def _head_norm(o, gain, lam_init):
    ms = jnp.mean(o * o, axis=-1, keepdims=True)
    return o * lax.rsqrt(ms + EPS) * gain * (1.0 - lam_init)


def _attn_prompt_kernel(q_ref, k_ref, v_ref, bdiag_ref, bsub_ref, lq_ref, sg_ref, o_ref,
                        k1_s, k2_s, v_s, qs_s, m_s, l_s, acc_s, *, tb, seq_len, lam_init):
    qi = pl.program_id(2)
    rows2 = GROUP * tb
    cast_rows = 512

    @pl.when(qi == 0)
    def _():
        lane = lax.broadcasted_iota(jnp.int32, (cast_rows, V_DIM), 1)

        def body(c, carry):
            rows = pl.ds(pl.multiple_of(c * cast_rows, cast_rows), cast_rows)
            kf = k_ref[0, rows, :]
            k1_s[rows, :] = jnp.where(lane < HEAD_DIM, kf, 0.0).astype(BF16)
            k2_s[rows, :] = jnp.where(lane >= HEAD_DIM, kf, 0.0).astype(BF16)
            v_s[rows, :] = v_ref[0, rows, :].astype(BF16)
            return carry

        lax.fori_loop(0, seq_len // cast_rows, body, 0)

    for g in range(GROUP):
        qs_s[g * tb:(g + 1) * tb, :] = q_ref[0, :, g * V_DIM:(g + 1) * V_DIM]
    m_s[...] = jnp.full(m_s.shape, NEG_INF, F32)
    l_s[...] = jnp.zeros(l_s.shape, F32)
    acc_s[...] = jnp.zeros(acc_s.shape, F32)

    def block(start, width, bias):
        rows = pl.ds(pl.multiple_of(start, tb), width)
        q = qs_s[...]
        vb = v_s[rows, :]
        for mi, ks in enumerate((k1_s, k2_s)):
            s = lax.dot_general(q, ks[rows, :], (((1,), (1,)), ((), ())),
                                preferred_element_type=F32)
            if bias is not None:
                s = s + bias
            cols = [s[:, c * LANES:(c + 1) * LANES] for c in range(width // LANES)]
            m_old = m_s[mi]
            m_new = jnp.maximum(m_old, jnp.max(functools.reduce(jnp.maximum, cols),
                                               axis=-1, keepdims=True))
            alpha = jnp.exp2(m_old - m_new)
            ps = [jnp.exp2(c - m_new) for c in cols]
            l_s[mi] = alpha * l_s[mi] + functools.reduce(jnp.add, ps)
            p = jnp.concatenate([x.astype(BF16) for x in ps], axis=1)
            acc_s[mi] = alpha * acc_s[mi] + jnp.dot(p, vb, preferred_element_type=F32)
            m_s[mi] = m_new

    def far(j, carry):
        block(j * tb, tb, None)
        return carry

    lax.fori_loop(0, qi - 1, far, 0)

    @pl.when(qi >= 1)
    def _():
        block((qi - 1) * tb, tb, bsub_ref[0])

    block(qi * tb, tb, bdiag_ref[0])

    lam = _lambda(lq_ref, lam_init)
    l1 = jnp.sum(l_s[0], axis=-1, keepdims=True)
    l2 = jnp.sum(l_s[1], axis=-1, keepdims=True)
    o = acc_s[0] / l1 - lam * (acc_s[1] / l2)
    o = _head_norm(o, sg_ref[...], lam_init)
    for g in range(GROUP):
        o_ref[0, :, g * V_DIM:(g + 1) * V_DIM] = o[g * tb:(g + 1) * tb, :].astype(o_ref.dtype)


def _attn_prompt(q, k, v, bdiag, bsub, lq, sg, *, lam_init):
    b, s, _ = q.shape
    tb = min(ATTN_TILE, s)
    rows2 = GROUP * tb
    kv_spec = pl.BlockSpec((1, s, V_DIM), lambda bi, h, qi: (bi, 0, h))
    q_spec = pl.BlockSpec((1, tb, GROUP * V_DIM), lambda bi, h, qi: (bi, qi, h))
    bias_spec = pl.BlockSpec((1, rows2, tb), lambda bi, h, qi: (h, 0, 0))
    return pl.pallas_call(
        functools.partial(_attn_prompt_kernel, tb=tb, seq_len=s, lam_init=lam_init),
        grid=(b, N_KV_HEADS, s // tb),
        in_specs=[q_spec, kv_spec, kv_spec, bias_spec, bias_spec,
                  _const_spec((4, HEAD_DIM)), _const_spec((1, V_DIM))],
        out_specs=q_spec,
        out_shape=jax.ShapeDtypeStruct((b, s, D_MODEL), BF16),
        scratch_shapes=[pltpu.VMEM((s, V_DIM), BF16), pltpu.VMEM((s, V_DIM), BF16),
                        pltpu.VMEM((s, V_DIM), BF16), pltpu.VMEM((rows2, V_DIM), BF16),
                        pltpu.VMEM((2, rows2, LANES), F32), pltpu.VMEM((2, rows2, LANES), F32),
                        pltpu.VMEM((2, rows2, V_DIM), F32)],
        compiler_params=_params(3),
        name="attn_prompt",
    )(q, k, v, bdiag, bsub, lq, sg)


def _prompt_bias(rel_bias, tb):
    by_dist = _bias_by_distance(rel_bias, 2 * tb - 1)
    masked = jnp.full((tb - 1, N_HEADS), NEG_INF, F32)
    rev = jnp.concatenate([masked, by_dist], axis=0)[::-1].T

    def tile(delta):
        rows = [rev[:, 2 * tb - 1 - delta - r:3 * tb - 1 - delta - r] for r in range(tb)]
        return jnp.stack(rows, axis=1).reshape(N_KV_HEADS, GROUP * tb, tb)

    return tile(0), tile(tb)


def _bias_by_distance(rel_bias, max_dist):
    tbl = (rel_bias - rel_bias[N_BUCKETS - 1:N_BUCKETS, :]) * LOG2E
    return tbl[_rel_buckets(max_dist), :]


def _attn_sample_kernel(pt_ref, q_ref, kn_ref, vn_ref, bias_ref, lq_ref, sg_ref, *rest,
                        n_pages, t_new, lam_init):
    k_refs = rest[:n_pages]
    v_refs = rest[n_pages:2 * n_pages]
    o_ref, kb_s, vb_s = rest[2 * n_pages:]
    past = n_pages * PAGE_SIZE

    qf = q_ref[0]
    lane = lax.broadcasted_iota(jnp.int32, (t_new, V_DIM), 1)
    pad = jnp.zeros((PAGE_SIZE - t_new, V_DIM), F32)
    lam = _lambda(lq_ref, lam_init)

    for h in range(N_KV_HEADS):
        head_rows = pl.ds(h, PAGE_SIZE, stride=N_KV_HEADS)
        for j in range(n_pages):
            dst = pl.ds(j * PAGE_SIZE, PAGE_SIZE)
            kb_s[dst, :] = k_refs[j][0, 0, head_rows, :].astype(BF16)
            vb_s[dst, :] = v_refs[j][0, 0, head_rows, :].astype(BF16)
        new = pl.ds(past, PAGE_SIZE)
        hcols = slice(h * V_DIM, (h + 1) * V_DIM)
        kb_s[new, :] = jnp.concatenate([kn_ref[0][:, hcols], pad], axis=0).astype(BF16)
        vb_s[new, :] = jnp.concatenate([vn_ref[0][:, hcols], pad], axis=0).astype(BF16)

        pieces = []
        for mi in range(2):
            keep = (lane < HEAD_DIM) if mi == 0 else (lane >= HEAD_DIM)
            for g in range(GROUP):
                c0 = (h * GROUP + g) * V_DIM
                pieces.append(jnp.where(keep, qf[:, c0:c0 + V_DIM], 0.0))
        wq = jnp.concatenate(pieces, axis=0).astype(BF16)

        s = lax.dot_general(wq, kb_s[...], (((1,), (1,)), ((), ())),
                            preferred_element_type=F32) + bias_ref[h]
        m = jnp.max(s, axis=-1, keepdims=True)
        p = jnp.exp2(s - m)
        l = jnp.sum(p, axis=-1, keepdims=True)
        acc = jnp.dot(p.astype(BF16), vb_s[...], preferred_element_type=F32) / l
        for g in range(GROUP):
            o1 = acc[g * t_new:(g + 1) * t_new, :]
            o2 = acc[(GROUP + g) * t_new:(GROUP + g + 1) * t_new, :]
            c0 = (h * GROUP + g) * V_DIM
            o_ref[0, :, c0:c0 + V_DIM] = _head_norm(o1 - lam * o2, sg_ref[...], lam_init)


def _attn_sample(page_table, q, kn, vn, bias, lq, sg, cache_k, cache_v, *, layer, lam_init):
    n, t_new, _ = q.shape
    n_pages = page_table.shape[1]
    width = N_KV_HEADS * V_DIM
    cols = (n_pages + 1) * PAGE_SIZE
    seq = lambda c: pl.BlockSpec((1, t_new, c), lambda i, pt: (i, 0, 0))
    const = lambda shape: pl.BlockSpec(shape, lambda i, pt: (0,) * len(shape),
                                       pipeline_mode=pl.Buffered(1))

    def page_spec(j):
        return pl.BlockSpec((1, 1, PAGE_SIZE * N_KV_HEADS, V_DIM),
                            lambda i, pt: (layer, pt[i * n_pages + j], 0, 0))

    pages = [page_spec(j) for j in range(n_pages)]
    grid_spec = pltpu.PrefetchScalarGridSpec(
        num_scalar_prefetch=1, grid=(n,),
        in_specs=[seq(D_MODEL), seq(width), seq(width),
                  const((N_KV_HEADS, 2 * GROUP * t_new, cols)),
                  const((4, HEAD_DIM)), const((1, V_DIM))] + pages + pages,
        out_specs=seq(D_MODEL),
        scratch_shapes=[pltpu.VMEM((cols, V_DIM), BF16), pltpu.VMEM((cols, V_DIM), BF16)])
    return pl.pallas_call(
        functools.partial(_attn_sample_kernel, n_pages=n_pages, t_new=t_new, lam_init=lam_init),
        grid_spec=grid_spec,
        out_shape=jax.ShapeDtypeStruct((n, t_new, D_MODEL), F32),
        compiler_params=_params(1),
        name="attn_sample",
    )(page_table.reshape(-1), q, kn, vn, bias, lq, sg,
      *([cache_k] * n_pages), *([cache_v] * n_pages))


def _sample_bias(rel_bias, n_pages, t_new):
    past = n_pages * PAGE_SIZE
    cols = past + PAGE_SIZE
    by_dist = _bias_by_distance(rel_bias, past + t_new - 1).T
    rows = []
    for qi in range(t_new):
        seen = by_dist[:, :past + qi + 1][:, ::-1]
        hidden = jnp.full((N_HEADS, cols - (past + qi + 1)), NEG_INF, F32)
        rows.append(jnp.concatenate([seen, hidden], axis=1))
    tile = jnp.stack(rows, axis=1)
    tile = tile.reshape(N_KV_HEADS, GROUP * t_new, cols)
    return jnp.concatenate([tile, tile], axis=1)


def _mixout_kernel(a_ref, cy_ref, sga_ref, sgc_ref, x_ref, wao_ref, wco_ref, wo_ref, o_ref):
    a = jnp.dot(a_ref[...].astype(BF16), wao_ref[...], preferred_element_type=F32)
    c = jnp.dot(cy_ref[...], wco_ref[...], preferred_element_type=F32)
    merged = sga_ref[...].astype(F32) * a + sgc_ref[...].astype(F32) * c
    o_ref[...] = x_ref[...] + jnp.dot(merged.astype(BF16), wo_ref[...],
                                      preferred_element_type=F32)


def _mixout(attn, cy, sga, sgc, x, wao, wco, wo):
    t = x.shape[0]
    tm = min(ROW_TILE, t)
    row = pl.BlockSpec((tm, D_MODEL), lambda i: (i, 0))
    w = _const_spec((D_MODEL, D_MODEL))
    return pl.pallas_call(
        _mixout_kernel, grid=(t // tm,),
        in_specs=[row, row, row, row, row, w, w, w], out_specs=row,
        out_shape=jax.ShapeDtypeStruct((t, D_MODEL), F32),
        compiler_params=_params(1), name="mixout",
    )(attn, cy, sga, sgc, x, wao, wco, wo)


def _ffn_kernel(x_ref, g_ref, wgu_ref, wd_ref, o_ref):
    x = x_ref[...]
    ms = jnp.mean(x * x, axis=-1, keepdims=True)
    h = (x * lax.rsqrt(ms + EPS) * g_ref[...]).astype(BF16)
    y = None
    for a, b in FFN_CHUNKS:
        gate = jnp.dot(h, wgu_ref[:, a:b], preferred_element_type=F32)
        up = jnp.dot(h, wgu_ref[:, FFN_DIM + a:FFN_DIM + b], preferred_element_type=F32)
        act = (gate * jax.nn.sigmoid(gate) * up).astype(BF16)
        part = jnp.dot(act, wd_ref[a:b, :], preferred_element_type=F32)
        y = part if y is None else y + part
    o_ref[...] = x + y


def _ffn(x, g, wgu, wd):
    t = x.shape[0]
    tm = min(ROW_TILE, t)
    row = pl.BlockSpec((tm, D_MODEL), lambda i: (i, 0))
    return pl.pallas_call(
        _ffn_kernel, grid=(t // tm,),
        in_specs=[row, _const_spec((1, D_MODEL)), _const_spec((D_MODEL, 2 * FFN_DIM)),
                  _const_spec((FFN_DIM, D_MODEL))],
        out_specs=row, out_shape=jax.ShapeDtypeStruct((t, D_MODEL), F32),
        compiler_params=_params(1), name="ffn",
    )(x, g, wgu, wd)


def kernel(x_prompt, x_sample, cache_k, cache_v, state_conv, page_table, rel_bias, mix_norm_g, w_in, q_norm_g, k_norm_g, lambda_qk, subln_g, conv_w, w_attn_out, w_conv_out, w_out, ffn_norm_g, w_gate_up, w_down):
    batch, seq, _ = x_prompt.shape
    n_dec, t_new, _ = x_sample.shape
    depth = w_in.shape[0]
    n_pool = cache_k.shape[1]
    n_pages = page_table.shape[1]
    width = N_KV_HEADS * V_DIM

    w_in_b, w_ao_b, w_co_b, w_o_b, w_gu_b, w_d_b = (
        w.astype(BF16) for w in (w_in, w_attn_out, w_conv_out, w_out, w_gate_up, w_down))
    blk = np.arange(D_MODEL) // HEAD_DIM
    bd = jnp.asarray(blk[:, None] == blk[None, :], BF16)
    cache_k4 = cache_k.reshape(depth, n_pool, PAGE_SIZE * N_KV_HEADS, V_DIM)
    cache_v4 = cache_v.reshape(depth, n_pool, PAGE_SIZE * N_KV_HEADS, V_DIM)

    tb = min(ATTN_TILE, seq)
    bdiag, bsub = _prompt_bias(rel_bias, tb)
    sbias = _sample_bias(rel_bias, n_pages, t_new)

    first = jnp.zeros((depth, n_dec, t_new - 1, D_MODEL), F32)
    s1 = jnp.concatenate([state_conv[:, :, 1:2], first], axis=2)
    s2 = jnp.concatenate([state_conv, first[:, :, 1:]], axis=2)
    s1 = s1.reshape(depth, n_dec * t_new, D_MODEL)
    s2 = s2.reshape(depth, n_dec * t_new, D_MODEL)

    xp = x_prompt.reshape(batch * seq, D_MODEL)
    xs = x_sample.reshape(n_dec * t_new, D_MODEL)
    tiles_per_seq = max(seq // min(ROW_TILE, batch * seq), 1)
    outs = [[] for _ in range(6)]
    for l in range(depth):
        lam_init = 0.8 - 0.6 * math.exp(-0.3 * l)
        g_mix = mix_norm_g[l].reshape(1, D_MODEL)
        qg = jnp.tile(q_norm_g[l], D_MODEL // HEAD_DIM).reshape(1, D_MODEL)
        kg = jnp.tile(k_norm_g[l], width // HEAD_DIM).reshape(1, width)
        sg = subln_g[l].reshape(1, V_DIM)
        lq = lambda_qk[l]

        q, k, v, cy, sga, sgc, cu = _inproj(xp, g_mix, w_in_b[l], bd, qg, kg, conv_w[l],
                                            seq_len=seq)
        attn = _attn_prompt(q.reshape(batch, seq, D_MODEL), k.reshape(batch, seq, width),
                            v.reshape(batch, seq, width), bdiag, bsub, lq, sg,
                            lam_init=lam_init)
        xp = _mixout(attn.reshape(batch * seq, D_MODEL), cy, sga, sgc, xp,
                     w_ao_b[l], w_co_b[l], w_o_b[l])
        xp = _ffn(xp, ffn_norm_g[l].reshape(1, D_MODEL), w_gu_b[l], w_d_b[l])
        outs[0].append(k.reshape(batch, seq, N_KV_HEADS, V_DIM))
        outs[1].append(v.reshape(batch, seq, N_KV_HEADS, V_DIM))
        tails = cu.reshape(batch, tiles_per_seq, SUBLANES, D_MODEL)
        outs[2].append(tails[:, -1, SUBLANES - (CONV_K - 1):, :])

        q, k, v, cy, sga, sgc, cu = _inproj(xs, g_mix, w_in_b[l], bd, qg, kg, conv_w[l],
                                            seq_len=t_new, state=(s1[l], s2[l]))
        attn = _attn_sample(page_table, q.reshape(n_dec, t_new, D_MODEL),
                            k.reshape(n_dec, t_new, width), v.reshape(n_dec, t_new, width),
                            sbias, lq, sg, cache_k4, cache_v4, layer=l, lam_init=lam_init)
        xs = _mixout(attn.reshape(n_dec * t_new, D_MODEL), cy, sga, sgc, xs,
                     w_ao_b[l], w_co_b[l], w_o_b[l])
        xs = _ffn(xs, ffn_norm_g[l].reshape(1, D_MODEL), w_gu_b[l], w_d_b[l])
        outs[3].append(k.reshape(n_dec, t_new, N_KV_HEADS, V_DIM))
        outs[4].append(v.reshape(n_dec, t_new, N_KV_HEADS, V_DIM))
        outs[5].append(cu.reshape(n_dec, t_new, D_MODEL)[:, t_new - (CONV_K - 1):, :])

    return (xp.reshape(batch, seq, D_MODEL), xs.reshape(n_dec, t_new, D_MODEL),
            jnp.stack(outs[0]), jnp.stack(outs[1]), jnp.stack(outs[2]),
            jnp.stack(outs[3]), jnp.stack(outs[4]), jnp.stack(outs[5]))
```

```python
import functools
import math

import jax
import jax.numpy as jnp
import numpy as np
from jax import lax
from jax.experimental import pallas as pl
from jax.experimental.pallas import tpu as pltpu

F32 = jnp.float32
BF16 = jnp.bfloat16

D_MODEL = 1024
N_HEADS = 8
N_KV_HEADS = 4
GROUP = N_HEADS // N_KV_HEADS
HEAD_DIM = 64
V_DIM = 2 * HEAD_DIM
ATTN_SCALE = HEAD_DIM ** -0.5
CONV_K = 3
FFN_DIM = 2816
N_BUCKETS = 32
MAX_EXACT = N_BUCKETS // 2
REL_MAX_DIST = 128
EPS = 1e-6
NEG_INF = -1e30
PAGE_SIZE = 128

OFF_Q, OFF_K, OFF_V, OFF_GB, OFF_GC, OFF_U, OFF_GA, OFF_GCV, IN_COLS = (
    0, 1024, 1536, 2048, 3072, 4096, 5120, 6144, 7168)

VMEM_LIMIT_BYTES = 56 * 1024 * 1024
SUBLANES = 8
LANES = 128
LOG2E = math.log2(math.e)
ROW_TILE = 512
ATTN_TILE = 512
FFN_CHUNKS = ((0, 1536), (1536, 2816))


def _params(n_axes, vmem=VMEM_LIMIT_BYTES):
    return pltpu.CompilerParams(
        dimension_semantics=("arbitrary",) * n_axes, vmem_limit_bytes=vmem)


def _const_spec(shape):
    zeros = (0,) * len(shape)
    return pl.BlockSpec(shape, lambda *_: zeros, pipeline_mode=pl.Buffered(1))


def _rel_buckets(max_dist):
    n = np.arange(max_dist + 1)
    nf = np.maximum(n, 1).astype(np.float32)
    large = MAX_EXACT + (np.log(nf / np.float32(MAX_EXACT)) / np.float32(math.log(REL_MAX_DIST / MAX_EXACT))
                         * np.float32(N_BUCKETS - MAX_EXACT)).astype(np.int32)
    large = np.minimum(large, N_BUCKETS - 1)
    return np.where(n < MAX_EXACT, n, large).astype(np.int32)


def _inproj_kernel(*refs, tm, tiles_per_seq, sample):
    if sample:
        (x_ref, g_ref, w_ref, bd_ref, qg_ref, kg_ref, cw_ref, s1_ref, s2_ref,
         q_ref, k_ref, v_ref, cy_ref, sga_ref, sgc_ref, cu_ref, ext) = refs
    else:
        (x_ref, g_ref, w_ref, bd_ref, qg_ref, kg_ref, cw_ref, _, _,
         q_ref, kz_ref, vb_ref, knew_ref, vnew_ref, cy_ref, sga_ref, sgc_ref, cu_ref,
         ext) = refs
    i = pl.program_id(0)

    x = x_ref[...]
    ms = jnp.mean(x * x, axis=-1, keepdims=True)
    h = (x * lax.rsqrt(ms + EPS) * g_ref[...]).astype(BF16)

    def proj(a, b):
        return jnp.dot(h, w_ref[:, a:b], preferred_element_type=F32)

    def group_norm(y, bd, gain):
        ss = jnp.dot((y * y).astype(BF16), bd, preferred_element_type=F32)
        return y * lax.rsqrt(ss * (1.0 / HEAD_DIM) + EPS) * gain

    yq = proj(OFF_Q, OFF_K)
    q_ref[...] = (group_norm(yq, bd_ref[...], qg_ref[...]) * (ATTN_SCALE * LOG2E)).astype(q_ref.dtype)
    k = group_norm(proj(OFF_K, OFF_V), bd_ref[0:512, 0:512], kg_ref[...])
    v = proj(OFF_V, OFF_GB)
    if sample:
        k_ref[...] = k
        v_ref[...] = v
    else:
        lane = lax.broadcasted_iota(jnp.int32, (tm, V_DIM), 1)
        for hd in range(N_KV_HEADS):
            kh = k[:, hd * V_DIM:(hd + 1) * V_DIM]
            vh = v[:, hd * V_DIM:(hd + 1) * V_DIM]
            head_rows = pl.ds(hd, tm, stride=N_KV_HEADS)
            knew_ref[head_rows, :] = kh
            vnew_ref[head_rows, :] = vh
            c0 = hd * 2 * V_DIM
            kz_ref[:, c0:c0 + V_DIM] = jnp.where(lane < HEAD_DIM, kh, 0.0).astype(BF16)
            kz_ref[:, c0 + V_DIM:c0 + 2 * V_DIM] = jnp.where(lane >= HEAD_DIM, kh, 0.0).astype(BF16)
            vb_ref[:, hd * V_DIM:(hd + 1) * V_DIM] = vh.astype(BF16)

    @pl.when(i % tiles_per_seq == 0)
    def _():
        ext[0:SUBLANES, :] = jnp.zeros((SUBLANES, D_MODEL), F32)

    cu = proj(OFF_GC, OFF_U) * proj(OFF_U, OFF_GA)
    ext[SUBLANES:SUBLANES + tm, :] = cu
    p1 = ext[SUBLANES - 1:SUBLANES - 1 + tm, :]
    p2 = ext[SUBLANES - 2:SUBLANES - 2 + tm, :]
    if sample:
        rmod = lax.broadcasted_iota(jnp.int32, (tm, D_MODEL), 0) & (SUBLANES - 1)
        p1 = jnp.where(rmod >= 1, p1, s1_ref[...])
        p2 = jnp.where(rmod >= 2, p2, s2_ref[...])
    cw = cw_ref[...]
    conv = cw[0:1, :] * p2 + cw[1:2, :] * p1 + cw[2:3, :] * cu
    cy_ref[...] = (proj(OFF_GB, OFF_GC) * conv).astype(BF16)
    if sample:
        cu_ref[...] = cu
    else:
        cu_ref[...] = cu[tm - SUBLANES:tm, :]
        ext[0:SUBLANES, :] = cu[tm - SUBLANES:tm, :]

    sga_ref[...] = jax.nn.sigmoid(proj(OFF_GA, OFF_GCV)).astype(BF16)
    sgc_ref[...] = jax.nn.sigmoid(proj(OFF_GCV, IN_COLS)).astype(BF16)


def _inproj(x, g, w_in, bd, qg, kg, cw, *, seq_len, state=None, new_kv=None, layer=0):
    t = x.shape[0]
    tm = min(ROW_TILE, t)
    n_tiles = t // tm
    sample = state is not None
    tiles_per_seq = max(seq_len // tm, 1)
    row = lambda c: pl.BlockSpec((tm, c), lambda i: (i, 0))
    in_specs = [row(D_MODEL), _const_spec((1, D_MODEL)), _const_spec((D_MODEL, IN_COLS)),
                _const_spec((D_MODEL, D_MODEL)), _const_spec((1, D_MODEL)),
                _const_spec((1, 512)), _const_spec((CONV_K, D_MODEL))]
    args = [x, g, w_in, bd, qg, kg, cw]
    act = lambda dt: jax.ShapeDtypeStruct((t, D_MODEL), dt)
    if sample:
        in_specs += [row(D_MODEL), row(D_MODEL)]
        args += list(state)
        out_shape = [act(F32), jax.ShapeDtypeStruct((t, 512), F32),
                     jax.ShapeDtypeStruct((t, 512), F32), act(BF16), act(BF16), act(BF16),
                     act(F32)]
        out_specs = [row(D_MODEL), row(512), row(512), row(D_MODEL), row(D_MODEL),
                     row(D_MODEL), row(D_MODEL)]
        aliases = {}
    else:
        in_specs += [pl.BlockSpec(memory_space=pl.ANY)] * 2
        args += list(new_kv)
        new_spec = pl.BlockSpec((tm * N_KV_HEADS, V_DIM), lambda i: (layer * n_tiles + i, 0))
        new_shape = jax.ShapeDtypeStruct(new_kv[0].shape, F32)
        out_shape = [act(BF16), act(BF16), jax.ShapeDtypeStruct((t, 512), BF16),
                     new_shape, new_shape, act(BF16), act(BF16), act(BF16),
                     jax.ShapeDtypeStruct((n_tiles * SUBLANES, D_MODEL), F32)]
        out_specs = [row(D_MODEL), row(D_MODEL), row(512), new_spec, new_spec,
                     row(D_MODEL), row(D_MODEL), row(D_MODEL),
                     pl.BlockSpec((SUBLANES, D_MODEL), lambda i: (i, 0))]
        aliases = {7: 3, 8: 4}
    return pl.pallas_call(
        functools.partial(_inproj_kernel, tm=tm, tiles_per_seq=tiles_per_seq, sample=sample),
        grid=(n_tiles,), in_specs=in_specs, out_specs=out_specs, out_shape=out_shape,
        scratch_shapes=[pltpu.VMEM((tm + SUBLANES, D_MODEL), F32)],
        input_output_aliases=aliases,
        compiler_params=_params(1),
        name="inproj_sample" if sample else "inproj_prompt",
    )(*args)


def _lambda(lq_ref, lam_init):
    lq = lq_ref[...]
    a = jnp.sum(lq[0:1, :] * lq[1:2, :], axis=-1, keepdims=True)
    b = jnp.sum(lq[2:3, :] * lq[3:4, :], axis=-1, keepdims=True)
    return jnp.exp(a) - jnp.exp(b) + lam_init


def _head_norm(o, gain, lam_init):
    ms = jnp.mean(o * o, axis=-1, keepdims=True)
    return o * lax.rsqrt(ms + EPS) * gain * (1.0 - lam_init)


def _attn_prompt_kernel(q_ref, kz_ref, vb_ref, bdiag_ref, bsub_ref, lq_ref, sg_ref, o_ref,
                        qs_s, m_s, l_s, acc_s, *, tb, lam_init):
    qi = pl.program_id(2)

    for g in range(GROUP):
        qs_s[g * tb:(g + 1) * tb, :] = q_ref[0, :, g * V_DIM:(g + 1) * V_DIM]
    m_s[...] = jnp.full(m_s.shape, NEG_INF, F32)
    l_s[...] = jnp.zeros(l_s.shape, F32)
    acc_s[...] = jnp.zeros(acc_s.shape, F32)

    def block(start, width, bias):
        rows = pl.ds(pl.multiple_of(start, tb), width)
        q = qs_s[...]
        vb = vb_ref[0, rows, :]
        for mi in range(2):
            kz = kz_ref[0, rows, mi * V_DIM:(mi + 1) * V_DIM]
            s = lax.dot_general(q, kz, (((1,), (1,)), ((), ())),
                                preferred_element_type=F32)
            if bias is not None:
                s = s + bias
            cols = [s[:, c * LANES:(c + 1) * LANES] for c in range(width // LANES)]
            m_old = m_s[mi]
            m_new = jnp.maximum(m_old, jnp.max(functools.reduce(jnp.maximum, cols),
                                               axis=-1, keepdims=True))
            alpha = jnp.exp2(m_old - m_new)
            ps = [jnp.exp2(c - m_new) for c in cols]
            l_s[mi] = alpha * l_s[mi] + functools.reduce(jnp.add, ps)
            p = jnp.concatenate([x.astype(BF16) for x in ps], axis=1)
            acc_s[mi] = alpha * acc_s[mi] + jnp.dot(p, vb, preferred_element_type=F32)
            m_s[mi] = m_new

    def far(j, carry):
        block(j * tb, tb, None)
        return carry

    lax.fori_loop(0, qi - 1, far, 0)

    @pl.when(qi >= 1)
    def _():
        block((qi - 1) * tb, tb, bsub_ref[0])

    block(qi * tb, tb, bdiag_ref[0])

    lam = _lambda(lq_ref, lam_init)
    l1 = jnp.sum(l_s[0], axis=-1, keepdims=True)
    l2 = jnp.sum(l_s[1], axis=-1, keepdims=True)
    o = acc_s[0] / l1 - lam * (acc_s[1] / l2)
    o = _head_norm(o, sg_ref[...], lam_init)
    for g in range(GROUP):
        o_ref[0, :, g * V_DIM:(g + 1) * V_DIM] = o[g * tb:(g + 1) * tb, :].astype(o_ref.dtype)


def _attn_prompt(q, kz, vb, bdiag, bsub, lq, sg, *, lam_init):
    b, s, _ = q.shape
    tb = min(ATTN_TILE, s)
    rows2 = GROUP * tb
    q_spec = pl.BlockSpec((1, tb, GROUP * V_DIM), lambda bi, h, qi: (bi, qi, h))
    kz_spec = pl.BlockSpec((1, s, 2 * V_DIM), lambda bi, h, qi: (bi, 0, h))
    vb_spec = pl.BlockSpec((1, s, V_DIM), lambda bi, h, qi: (bi, 0, h))
    bias_spec = pl.BlockSpec((1, rows2, tb), lambda bi, h, qi: (h, 0, 0))
    return pl.pallas_call(
        functools.partial(_attn_prompt_kernel, tb=tb, lam_init=lam_init),
        grid=(b, N_KV_HEADS, s // tb),
        in_specs=[q_spec, kz_spec, vb_spec, bias_spec, bias_spec,
                  _const_spec((4, HEAD_DIM)), _const_spec((1, V_DIM))],
        out_specs=q_spec,
        out_shape=jax.ShapeDtypeStruct((b, s, D_MODEL), BF16),
        scratch_shapes=[pltpu.VMEM((rows2, V_DIM), BF16),
                        pltpu.VMEM((2, rows2, LANES), F32), pltpu.VMEM((2, rows2, LANES), F32),
                        pltpu.VMEM((2, rows2, V_DIM), F32)],
        compiler_params=_params(3),
        name="attn_prompt",
    )(q, kz, vb, bdiag, bsub, lq, sg)


def _prompt_bias(rel_bias, tb):
    by_dist = _bias_by_distance(rel_bias, 2 * tb - 1).T
    masked = jnp.full((N_HEADS, tb), NEG_INF, F32)

    def first_row(delta):
        if delta == 0:
            left = jnp.concatenate([by_dist[:, 0:1], masked[:, 1:]], axis=1)
        else:
            left = by_dist[:, delta:delta - tb:-1]
        right = by_dist[:, delta + 1:delta + tb][:, ::-1]
        return jnp.concatenate([left, masked[:, 0:1], right], axis=1)

    rows = jnp.stack([first_row(0), first_row(tb)], axis=1)
    tile = jax.ShapeDtypeStruct((N_HEADS, tb, tb), F32)
    tile_spec = pl.BlockSpec((1, tb, tb), lambda h: (h, 0, 0))
    bdiag, bsub = pl.pallas_call(
        functools.partial(_toeplitz_kernel, tb=tb), grid=(N_HEADS,),
        in_specs=[pl.BlockSpec((1, 2, 2 * tb), lambda h: (h, 0, 0))],
        out_specs=[tile_spec, tile_spec], out_shape=[tile, tile],
        compiler_params=_params(1), name="bias_tiles",
    )(rows)
    shape = (N_KV_HEADS, GROUP * tb, tb)
    return bdiag.reshape(shape), bsub.reshape(shape)


def _toeplitz_kernel(v_ref, diag_ref, sub_ref, *, tb):
    for t, o_ref in enumerate((diag_ref, sub_ref)):
        x = jnp.broadcast_to(v_ref[0, t:t + 1, :], (tb, 2 * tb))
        o_ref[0] = pltpu.roll(x, 0, 1, stride=1, stride_axis=0)[:, :tb]


def _bias_by_distance(rel_bias, max_dist):
    tbl = (rel_bias - rel_bias[N_BUCKETS - 1:N_BUCKETS, :]) * LOG2E
    return tbl[_rel_buckets(max_dist), :]


def _attn_sample_kernel(pt_ref, q_ref, kn_ref, vn_ref, bias_ref, lq_ref, sg_ref, *rest,
                        n_pages, t_new, lam_init):
    k_refs = rest[:n_pages]
    v_refs = rest[n_pages:2 * n_pages]
    o_ref, kb_s, vb_s = rest[2 * n_pages:]
    past = n_pages * PAGE_SIZE

    qf = q_ref[0]
    lane = lax.broadcasted_iota(jnp.int32, (t_new, V_DIM), 1)
    pad = jnp.zeros((PAGE_SIZE - t_new, V_DIM), F32)
    lam = _lambda(lq_ref, lam_init)

    for h in range(N_KV_HEADS):
        head_rows = pl.ds(h, PAGE_SIZE, stride=N_KV_HEADS)
        for j in range(n_pages):
            dst = pl.ds(j * PAGE_SIZE, PAGE_SIZE)
            kb_s[h, dst, :] = k_refs[j][0, 0, head_rows, :].astype(BF16)
            vb_s[h, dst, :] = v_refs[j][0, 0, head_rows, :].astype(BF16)
        new = pl.ds(past, PAGE_SIZE)
        hcols = slice(h * V_DIM, (h + 1) * V_DIM)
        kb_s[h, new, :] = jnp.concatenate([kn_ref[0][:, hcols], pad], axis=0).astype(BF16)
        vb_s[h, new, :] = jnp.concatenate([vn_ref[0][:, hcols], pad], axis=0).astype(BF16)

        pieces = []
        for mi in range(2):
            keep = (lane < HEAD_DIM) if mi == 0 else (lane >= HEAD_DIM)
            for g in range(GROUP):
                c0 = (h * GROUP + g) * V_DIM
                pieces.append(jnp.where(keep, qf[:, c0:c0 + V_DIM], 0.0))
        wq = jnp.concatenate(pieces, axis=0).astype(BF16)

        s = lax.dot_general(wq, kb_s[h], (((1,), (1,)), ((), ())),
                            preferred_element_type=F32) + bias_ref[h]
        m = jnp.max(s, axis=-1, keepdims=True)
        p = jnp.exp2(s - m)
        l = jnp.sum(p, axis=-1, keepdims=True)
        acc = jnp.dot(p.astype(BF16), vb_s[h], preferred_element_type=F32) / l
        for g in range(GROUP):
            o1 = acc[g * t_new:(g + 1) * t_new, :]
            o2 = acc[(GROUP + g) * t_new:(GROUP + g + 1) * t_new, :]
            c0 = (h * GROUP + g) * V_DIM
            o_ref[0, :, c0:c0 + V_DIM] = _head_norm(o1 - lam * o2, sg_ref[...], lam_init)


def _attn_sample(page_table, q, kn, vn, bias, lq, sg, cache_k, cache_v, *, layer, lam_init):
    n, t_new, _ = q.shape
    n_pages = page_table.shape[1]
    width = N_KV_HEADS * V_DIM
    cols = (n_pages + 1) * PAGE_SIZE
    seq = lambda c: pl.BlockSpec((1, t_new, c), lambda i, pt: (i, 0, 0))
    const = lambda shape: pl.BlockSpec(shape, lambda i, pt: (0,) * len(shape),
                                       pipeline_mode=pl.Buffered(1))

    def page_spec(j):
        return pl.BlockSpec((1, 1, PAGE_SIZE * N_KV_HEADS, V_DIM),
                            lambda i, pt: (layer, pt[i * n_pages + j], 0, 0))

    pages = [page_spec(j) for j in range(n_pages)]
    grid_spec = pltpu.PrefetchScalarGridSpec(
        num_scalar_prefetch=1, grid=(n,),
        in_specs=[seq(D_MODEL), seq(width), seq(width),
                  const((N_KV_HEADS, 2 * GROUP * t_new, cols)),
                  const((4, HEAD_DIM)), const((1, V_DIM))] + pages + pages,
        out_specs=seq(D_MODEL),
        scratch_shapes=[pltpu.VMEM((N_KV_HEADS, cols, V_DIM), BF16),
                        pltpu.VMEM((N_KV_HEADS, cols, V_DIM), BF16)])
    return pl.pallas_call(
        functools.partial(_attn_sample_kernel, n_pages=n_pages, t_new=t_new, lam_init=lam_init),
        grid_spec=grid_spec,
        out_shape=jax.ShapeDtypeStruct((n, t_new, D_MODEL), F32),
        compiler_params=_params(1),
        name="attn_sample",
    )(page_table.reshape(-1), q, kn, vn, bias, lq, sg,
      *([cache_k] * n_pages), *([cache_v] * n_pages))


def _sample_bias(rel_bias, n_pages, t_new):
    past = n_pages * PAGE_SIZE
    cols = past + PAGE_SIZE
    by_dist = _bias_by_distance(rel_bias, past + t_new - 1).T
    rows = []
    for qi in range(t_new):
        seen = by_dist[:, :past + qi + 1][:, ::-1]
        hidden = jnp.full((N_HEADS, cols - (past + qi + 1)), NEG_INF, F32)
        rows.append(jnp.concatenate([seen, hidden], axis=1))
    tile = jnp.stack(rows, axis=1)
    tile = tile.reshape(N_KV_HEADS, GROUP * t_new, cols)
    return jnp.concatenate([tile, tile], axis=1)


def _mixout_kernel(a_ref, cy_ref, sga_ref, sgc_ref, x_ref, wao_ref, wco_ref, wo_ref, o_ref):
    a = jnp.dot(a_ref[...].astype(BF16), wao_ref[...], preferred_element_type=F32)
    c = jnp.dot(cy_ref[...], wco_ref[...], preferred_element_type=F32)
    merged = sga_ref[...].astype(F32) * a + sgc_ref[...].astype(F32) * c
    o_ref[...] = x_ref[...] + jnp.dot(merged.astype(BF16), wo_ref[...],
                                      preferred_element_type=F32)


def _mixout(attn, cy, sga, sgc, x, wao, wco, wo):
    t = x.shape[0]
    tm = min(ROW_TILE, t)
    row = pl.BlockSpec((tm, D_MODEL), lambda i: (i, 0))
    w = _const_spec((D_MODEL, D_MODEL))
    return pl.pallas_call(
        _mixout_kernel, grid=(t // tm,),
        in_specs=[row, row, row, row, row, w, w, w], out_specs=row,
        out_shape=jax.ShapeDtypeStruct((t, D_MODEL), F32),
        compiler_params=_params(1), name="mixout",
    )(attn, cy, sga, sgc, x, wao, wco, wo)


def _ffn_kernel(x_ref, g_ref, wgu_ref, wd_ref, o_ref):
    x = x_ref[...]
    ms = jnp.mean(x * x, axis=-1, keepdims=True)
    h = (x * lax.rsqrt(ms + EPS) * g_ref[...]).astype(BF16)
    y = None
    for a, b in FFN_CHUNKS:
        gate = jnp.dot(h, wgu_ref[:, a:b], preferred_element_type=F32)
        up = jnp.dot(h, wgu_ref[:, FFN_DIM + a:FFN_DIM + b], preferred_element_type=F32)
        act = (gate * jax.nn.sigmoid(gate) * up).astype(BF16)
        part = jnp.dot(act, wd_ref[a:b, :], preferred_element_type=F32)
        y = part if y is None else y + part
    o_ref[...] = x + y


def _ffn(x, g, wgu, wd):
    t = x.shape[0]
    tm = min(ROW_TILE, t)
    row = pl.BlockSpec((tm, D_MODEL), lambda i: (i, 0))
    return pl.pallas_call(
        _ffn_kernel, grid=(t // tm,),
        in_specs=[row, _const_spec((1, D_MODEL)), _const_spec((D_MODEL, 2 * FFN_DIM)),
                  _const_spec((FFN_DIM, D_MODEL))],
        out_specs=row, out_shape=jax.ShapeDtypeStruct((t, D_MODEL), F32),
        compiler_params=_params(1), name="ffn",
    )(x, g, wgu, wd)


def kernel(x_prompt, x_sample, cache_k, cache_v, state_conv, page_table, rel_bias, mix_norm_g, w_in, q_norm_g, k_norm_g, lambda_qk, subln_g, conv_w, w_attn_out, w_conv_out, w_out, ffn_norm_g, w_gate_up, w_down):
    batch, seq, _ = x_prompt.shape
    n_dec, t_new, _ = x_sample.shape
    depth = w_in.shape[0]
    n_pool = cache_k.shape[1]
    n_pages = page_table.shape[1]
    width = N_KV_HEADS * V_DIM

    w_in_b, w_ao_b, w_co_b, w_o_b, w_gu_b, w_d_b = (
        w.astype(BF16) for w in (w_in, w_attn_out, w_conv_out, w_out, w_gate_up, w_down))
    blk = np.arange(D_MODEL) // HEAD_DIM
    bd = jnp.asarray(blk[:, None] == blk[None, :], BF16)
    cache_k4 = cache_k.reshape(depth, n_pool, PAGE_SIZE * N_KV_HEADS, V_DIM)
    cache_v4 = cache_v.reshape(depth, n_pool, PAGE_SIZE * N_KV_HEADS, V_DIM)

    tb = min(ATTN_TILE, seq)
    bdiag, bsub = _prompt_bias(rel_bias, tb)
    sbias = _sample_bias(rel_bias, n_pages, t_new)

    first = jnp.zeros((depth, n_dec, t_new - 1, D_MODEL), F32)
    s1 = jnp.concatenate([state_conv[:, :, 1:2], first], axis=2)
    s2 = jnp.concatenate([state_conv, first[:, :, 1:]], axis=2)
    s1 = s1.reshape(depth, n_dec * t_new, D_MODEL)
    s2 = s2.reshape(depth, n_dec * t_new, D_MODEL)

    xp = x_prompt.reshape(batch * seq, D_MODEL)
    xs = x_sample.reshape(n_dec * t_new, D_MODEL)
    tiles_per_seq = max(seq // min(ROW_TILE, batch * seq), 1)
    conv_p, k_s, v_s, conv_s = [], [], [], []
    new_k = jnp.zeros((depth * batch * seq * N_KV_HEADS, V_DIM), F32)
    new_v = jnp.zeros((depth * batch * seq * N_KV_HEADS, V_DIM), F32)
    for l in range(depth):
        lam_init = 0.8 - 0.6 * math.exp(-0.3 * l)
        g_mix = mix_norm_g[l].reshape(1, D_MODEL)
        qg = jnp.tile(q_norm_g[l], D_MODEL // HEAD_DIM).reshape(1, D_MODEL)
        kg = jnp.tile(k_norm_g[l], width // HEAD_DIM).reshape(1, width)
        sg = subln_g[l].reshape(1, V_DIM)
        lq = lambda_qk[l]

        q, kz, vb, new_k, new_v, cy, sga, sgc, cu = _inproj(
            xp, g_mix, w_in_b[l], bd, qg, kg, conv_w[l], seq_len=seq,
            new_kv=(new_k, new_v), layer=l)
        attn = _attn_prompt(q.reshape(batch, seq, D_MODEL), kz.reshape(batch, seq, D_MODEL),
                            vb.reshape(batch, seq, width), bdiag, bsub, lq, sg,
                            lam_init=lam_init)
        xp = _mixout(attn.reshape(batch * seq, D_MODEL), cy, sga, sgc, xp,
                     w_ao_b[l], w_co_b[l], w_o_b[l])
        xp = _ffn(xp, ffn_norm_g[l].reshape(1, D_MODEL), w_gu_b[l], w_d_b[l])
        tails = cu.reshape(batch, tiles_per_seq, SUBLANES, D_MODEL)
        conv_p.append(tails[:, -1, SUBLANES - (CONV_K - 1):, :])

        q, k, v, cy, sga, sgc, cu = _inproj(xs, g_mix, w_in_b[l], bd, qg, kg, conv_w[l],
                                            seq_len=t_new, state=(s1[l], s2[l]))
        attn = _attn_sample(page_table, q.reshape(n_dec, t_new, D_MODEL),
                            k.reshape(n_dec, t_new, width), v.reshape(n_dec, t_new, width),
                            sbias, lq, sg, cache_k4, cache_v4, layer=l, lam_init=lam_init)
        xs = _mixout(attn.reshape(n_dec * t_new, D_MODEL), cy, sga, sgc, xs,
                     w_ao_b[l], w_co_b[l], w_o_b[l])
        xs = _ffn(xs, ffn_norm_g[l].reshape(1, D_MODEL), w_gu_b[l], w_d_b[l])
        k_s.append(k.reshape(n_dec, t_new, N_KV_HEADS, V_DIM))
        v_s.append(v.reshape(n_dec, t_new, N_KV_HEADS, V_DIM))
        conv_s.append(cu.reshape(n_dec, t_new, D_MODEL)[:, t_new - (CONV_K - 1):, :])

    return (xp.reshape(batch, seq, D_MODEL), xs.reshape(n_dec, t_new, D_MODEL),
            new_k.reshape(depth, batch, seq, N_KV_HEADS, V_DIM),
            new_v.reshape(depth, batch, seq, N_KV_HEADS, V_DIM), jnp.stack(conv_p),
            jnp.stack(k_s), jnp.stack(v_s), jnp.stack(conv_s))
```

```python
import functools
import math

import jax
import jax.numpy as jnp
import numpy as np
from jax import lax
from jax.experimental import pallas as pl
from jax.experimental.pallas import tpu as pltpu

F32 = jnp.float32
BF16 = jnp.bfloat16

D_MODEL = 1024
N_HEADS = 8
N_KV_HEADS = 4
GROUP = N_HEADS // N_KV_HEADS
HEAD_DIM = 64
V_DIM = 2 * HEAD_DIM
ATTN_SCALE = HEAD_DIM ** -0.5
CONV_K = 3
FFN_DIM = 2816
N_BUCKETS = 32
MAX_EXACT = N_BUCKETS // 2
REL_MAX_DIST = 128
EPS = 1e-6
NEG_INF = -1e30
PAGE_SIZE = 128

OFF_Q, OFF_K, OFF_V, OFF_GB, OFF_GC, OFF_U, OFF_GA, OFF_GCV, IN_COLS = (
    0, 1024, 1536, 2048, 3072, 4096, 5120, 6144, 7168)

VMEM_LIMIT_BYTES = 56 * 1024 * 1024
SUBLANES = 8
LANES = 128
LOG2E = math.log2(math.e)
ROW_TILE = 512
ATTN_TILE = 512
FFN_CHUNKS = ((0, 1536), (1536, 2816))


def _params(n_axes, vmem=VMEM_LIMIT_BYTES):
    return pltpu.CompilerParams(
        dimension_semantics=("arbitrary",) * n_axes, vmem_limit_bytes=vmem)


def _const_spec(shape):
    zeros = (0,) * len(shape)
    return pl.BlockSpec(shape, lambda *_: zeros, pipeline_mode=pl.Buffered(1))


def _layer_spec(shape, layer):
    zeros = (0,) * len(shape)
    return pl.BlockSpec((None,) + tuple(shape), lambda *_: (layer,) + zeros,
                        pipeline_mode=pl.Buffered(1))


def _rel_buckets(max_dist):
    n = np.arange(max_dist + 1)
    nf = np.maximum(n, 1).astype(np.float32)
    large = MAX_EXACT + (np.log(nf / np.float32(MAX_EXACT)) / np.float32(math.log(REL_MAX_DIST / MAX_EXACT))
                         * np.float32(N_BUCKETS - MAX_EXACT)).astype(np.int32)
    large = np.minimum(large, N_BUCKETS - 1)
    return np.where(n < MAX_EXACT, n, large).astype(np.int32)


def _inproj_kernel(*refs, tm, tiles_per_seq, sample):
    if sample:
        (x_ref, g_ref, w_ref, bd_ref, qg_ref, kg_ref, cw_ref, s1_ref, s2_ref,
         q_ref, k_ref, v_ref, cy_ref, sga_ref, sgc_ref, cu_ref, ext) = refs
    else:
        (x_ref, g_ref, w_ref, bd_ref, qg_ref, kg_ref, cw_ref, _, _,
         q_ref, kz_ref, vb_ref, knew_ref, vnew_ref, cy_ref, sga_ref, sgc_ref, cu_ref,
         ext) = refs
    i = pl.program_id(0)

    x = x_ref[...]
    ms = jnp.mean(x * x, axis=-1, keepdims=True)
    h = (x * lax.rsqrt(ms + EPS) * g_ref[...]).astype(BF16)

    def proj(a, b):
        return jnp.dot(h, w_ref[:, a:b], preferred_element_type=F32)

    def group_norm(y, bd, gain):
        ss = jnp.dot((y * y).astype(BF16), bd, preferred_element_type=F32)
        return y * lax.rsqrt(ss * (1.0 / HEAD_DIM) + EPS) * gain

    yq = proj(OFF_Q, OFF_K)
    q_ref[...] = (group_norm(yq, bd_ref[...], qg_ref[...]) * (ATTN_SCALE * LOG2E)).astype(q_ref.dtype)
    k = group_norm(proj(OFF_K, OFF_V), bd_ref[0:512, 0:512], kg_ref[...])
    v = proj(OFF_V, OFF_GB)
    if sample:
        k_ref[...] = k
        v_ref[...] = v
    else:
        lane = lax.broadcasted_iota(jnp.int32, (tm, V_DIM), 1)
        for hd in range(N_KV_HEADS):
            kh = k[:, hd * V_DIM:(hd + 1) * V_DIM]
            vh = v[:, hd * V_DIM:(hd + 1) * V_DIM]
            head_rows = pl.ds(hd, tm, stride=N_KV_HEADS)
            knew_ref[head_rows, :] = kh
            vnew_ref[head_rows, :] = vh
            c0 = hd * 2 * V_DIM
            kz_ref[:, c0:c0 + V_DIM] = jnp.where(lane < HEAD_DIM, kh, 0.0).astype(BF16)
            kz_ref[:, c0 + V_DIM:c0 + 2 * V_DIM] = jnp.where(lane >= HEAD_DIM, kh, 0.0).astype(BF16)
            vb_ref[:, hd * V_DIM:(hd + 1) * V_DIM] = vh.astype(BF16)

    @pl.when(i % tiles_per_seq == 0)
    def _():
        ext[0:SUBLANES, :] = jnp.zeros((SUBLANES, D_MODEL), F32)

    cu = proj(OFF_GC, OFF_U) * proj(OFF_U, OFF_GA)
    ext[SUBLANES:SUBLANES + tm, :] = cu
    p1 = ext[SUBLANES - 1:SUBLANES - 1 + tm, :]
    p2 = ext[SUBLANES - 2:SUBLANES - 2 + tm, :]
    if sample:
        rmod = lax.broadcasted_iota(jnp.int32, (tm, D_MODEL), 0) & (SUBLANES - 1)
        p1 = jnp.where(rmod >= 1, p1, s1_ref[...])
        p2 = jnp.where(rmod >= 2, p2, s2_ref[...])
    cw = cw_ref[...]
    conv = cw[0:1, :] * p2 + cw[1:2, :] * p1 + cw[2:3, :] * cu
    cy_ref[...] = (proj(OFF_GB, OFF_GC) * conv).astype(BF16)
    if sample:
        cu_ref[...] = cu
    else:
        cu_ref[...] = cu[tm - SUBLANES:tm, :]
        ext[0:SUBLANES, :] = cu[tm - SUBLANES:tm, :]

    sga_ref[...] = jax.nn.sigmoid(proj(OFF_GA, OFF_GCV)).astype(BF16)
    sgc_ref[...] = jax.nn.sigmoid(proj(OFF_GCV, IN_COLS)).astype(BF16)


def _inproj(x, g, w_in, bd, qg, kg, cw, *, seq_len, state=None, new_kv=None, layer=0):
    t = x.shape[0]
    tm = min(ROW_TILE, t)
    n_tiles = t // tm
    sample = state is not None
    tiles_per_seq = max(seq_len // tm, 1)
    row = lambda c: pl.BlockSpec((tm, c), lambda i: (i, 0))
    in_specs = [row(D_MODEL), _const_spec((1, D_MODEL)), _layer_spec((D_MODEL, IN_COLS), layer),
                _const_spec((D_MODEL, D_MODEL)), _const_spec((1, D_MODEL)),
                _const_spec((1, 512)), _const_spec((CONV_K, D_MODEL))]
    args = [x, g, w_in, bd, qg, kg, cw]
    act = lambda dt: jax.ShapeDtypeStruct((t, D_MODEL), dt)
    if sample:
        in_specs += [row(D_MODEL), row(D_MODEL)]
        args += list(state)
        out_shape = [act(F32), jax.ShapeDtypeStruct((t, 512), F32),
                     jax.ShapeDtypeStruct((t, 512), F32), act(BF16), act(BF16), act(BF16),
                     act(F32)]
        out_specs = [row(D_MODEL), row(512), row(512), row(D_MODEL), row(D_MODEL),
                     row(D_MODEL), row(D_MODEL)]
        aliases = {}
    else:
        in_specs += [pl.BlockSpec(memory_space=pl.ANY)] * 2
        args += list(new_kv)
        new_spec = pl.BlockSpec((tm * N_KV_HEADS, V_DIM), lambda i: (layer * n_tiles + i, 0))
        new_shape = jax.ShapeDtypeStruct(new_kv[0].shape, F32)
        out_shape = [act(BF16), act(BF16), jax.ShapeDtypeStruct((t, 512), BF16),
                     new_shape, new_shape, act(BF16), act(BF16), act(BF16),
                     jax.ShapeDtypeStruct((n_tiles * SUBLANES, D_MODEL), F32)]
        out_specs = [row(D_MODEL), row(D_MODEL), row(512), new_spec, new_spec,
                     row(D_MODEL), row(D_MODEL), row(D_MODEL),
                     pl.BlockSpec((SUBLANES, D_MODEL), lambda i: (i, 0))]
        aliases = {7: 3, 8: 4}
    return pl.pallas_call(
        functools.partial(_inproj_kernel, tm=tm, tiles_per_seq=tiles_per_seq, sample=sample),
        grid=(n_tiles,), in_specs=in_specs, out_specs=out_specs, out_shape=out_shape,
        scratch_shapes=[pltpu.VMEM((tm + SUBLANES, D_MODEL), F32)],
        input_output_aliases=aliases,
        compiler_params=_params(1),
        name="inproj_sample" if sample else "inproj_prompt",
    )(*args)


def _lambda(lq_ref, lam_init):
    lq = lq_ref[...]
    a = jnp.sum(lq[0:1, :] * lq[1:2, :], axis=-1, keepdims=True)
    b = jnp.sum(lq[2:3, :] * lq[3:4, :], axis=-1, keepdims=True)
    return jnp.exp(a) - jnp.exp(b) + lam_init


def _head_norm(o, gain, lam_init):
    ms = jnp.mean(o * o, axis=-1, keepdims=True)
    return o * lax.rsqrt(ms + EPS) * gain * (1.0 - lam_init)


def _attn_prompt_kernel(q_ref, kz_ref, vb_ref, bdiag_ref, bsub_ref, lq_ref, sg_ref, o_ref,
                        qs_s, m_s, l_s, acc_s, *, tb, lam_init):
    qi = pl.program_id(2)

    for g in range(GROUP):
        qs_s[g * tb:(g + 1) * tb, :] = q_ref[0, :, g * V_DIM:(g + 1) * V_DIM]
    m_s[...] = jnp.full(m_s.shape, NEG_INF, F32)
    l_s[...] = jnp.zeros(l_s.shape, F32)
    acc_s[...] = jnp.zeros(acc_s.shape, F32)

    def block(start, width, bias):
        rows = pl.ds(pl.multiple_of(start, tb), width)
        q = qs_s[...]
        vb = vb_ref[0, rows, :]
        for mi in range(2):
            kz = kz_ref[0, rows, mi * V_DIM:(mi + 1) * V_DIM]
            s = lax.dot_general(q, kz, (((1,), (1,)), ((), ())),
                                preferred_element_type=F32)
            if bias is not None:
                s = s + bias
            cols = [s[:, c * LANES:(c + 1) * LANES] for c in range(width // LANES)]
            m_old = m_s[mi]
            m_new = jnp.maximum(m_old, jnp.max(functools.reduce(jnp.maximum, cols),
                                               axis=-1, keepdims=True))
            alpha = jnp.exp2(m_old - m_new)
            ps = [jnp.exp2(c - m_new) for c in cols]
            l_s[mi] = alpha * l_s[mi] + functools.reduce(jnp.add, ps)
            p = jnp.concatenate([x.astype(BF16) for x in ps], axis=1)
            acc_s[mi] = alpha * acc_s[mi] + jnp.dot(p, vb, preferred_element_type=F32)
            m_s[mi] = m_new

    n_far = jnp.maximum(qi - 1, 0)

    def far(j, carry):
        block(j * (2 * tb), 2 * tb, None)
        return carry

    lax.fori_loop(0, n_far // 2, far, 0)

    @pl.when(n_far % 2 == 1)
    def _():
        block((n_far - 1) * tb, tb, None)

    @pl.when(qi >= 1)
    def _():
        block((qi - 1) * tb, tb, bsub_ref[0])

    block(qi * tb, tb, bdiag_ref[0])

    lam = _lambda(lq_ref, lam_init)
    l1 = jnp.sum(l_s[0], axis=-1, keepdims=True)
    l2 = jnp.sum(l_s[1], axis=-1, keepdims=True)
    o = acc_s[0] / l1 - lam * (acc_s[1] / l2)
    o = _head_norm(o, sg_ref[...], lam_init)
    for g in range(GROUP):
        o_ref[0, :, g * V_DIM:(g + 1) * V_DIM] = o[g * tb:(g + 1) * tb, :].astype(o_ref.dtype)


def _attn_prompt(q, kz, vb, bdiag, bsub, lq, sg, *, lam_init):
    b, s, _ = q.shape
    tb = min(ATTN_TILE, s)
    rows2 = GROUP * tb
    q_spec = pl.BlockSpec((1, tb, GROUP * V_DIM), lambda bi, h, qi: (bi, qi, h))
    kz_spec = pl.BlockSpec((1, s, 2 * V_DIM), lambda bi, h, qi: (bi, 0, h))
    vb_spec = pl.BlockSpec((1, s, V_DIM), lambda bi, h, qi: (bi, 0, h))
    bias_spec = pl.BlockSpec((1, rows2, tb), lambda bi, h, qi: (h, 0, 0))
    return pl.pallas_call(
        functools.partial(_attn_prompt_kernel, tb=tb, lam_init=lam_init),
        grid=(b, N_KV_HEADS, s // tb),
        in_specs=[q_spec, kz_spec, vb_spec, bias_spec, bias_spec,
                  _const_spec((4, HEAD_DIM)), _const_spec((1, V_DIM))],
        out_specs=q_spec,
        out_shape=jax.ShapeDtypeStruct((b, s, D_MODEL), BF16),
        scratch_shapes=[pltpu.VMEM((rows2, V_DIM), BF16),
                        pltpu.VMEM((2, rows2, LANES), F32), pltpu.VMEM((2, rows2, LANES), F32),
                        pltpu.VMEM((2, rows2, V_DIM), F32)],
        compiler_params=_params(3),
        name="attn_prompt",
    )(q, kz, vb, bdiag, bsub, lq, sg)


def _prompt_bias(rel_bias, tb):
    by_dist = _bias_by_distance(rel_bias, 2 * tb - 1).T
    masked = jnp.full((N_HEADS, tb), NEG_INF, F32)

    def first_row(delta):
        if delta == 0:
            left = jnp.concatenate([by_dist[:, 0:1], masked[:, 1:]], axis=1)
        else:
            left = by_dist[:, delta:delta - tb:-1]
        right = by_dist[:, delta + 1:delta + tb][:, ::-1]
        return jnp.concatenate([left, masked[:, 0:1], right], axis=1)

    rows = jnp.stack([first_row(0), first_row(tb)], axis=1)
    tile = jax.ShapeDtypeStruct((N_HEADS, tb, tb), F32)
    tile_spec = pl.BlockSpec((1, tb, tb), lambda h: (h, 0, 0))
    bdiag, bsub = pl.pallas_call(
        functools.partial(_toeplitz_kernel, tb=tb), grid=(N_HEADS,),
        in_specs=[pl.BlockSpec((1, 2, 2 * tb), lambda h: (h, 0, 0))],
        out_specs=[tile_spec, tile_spec], out_shape=[tile, tile],
        compiler_params=_params(1), name="bias_tiles",
    )(rows)
    shape = (N_KV_HEADS, GROUP * tb, tb)
    return bdiag.reshape(shape), bsub.reshape(shape)


def _toeplitz_kernel(v_ref, diag_ref, sub_ref, *, tb):
    for t, o_ref in enumerate((diag_ref, sub_ref)):
        x = jnp.broadcast_to(v_ref[0, t:t + 1, :], (tb, 2 * tb))
        o_ref[0] = pltpu.roll(x, 0, 1, stride=1, stride_axis=0)[:, :tb]


def _bias_by_distance(rel_bias, max_dist):
    tbl = (rel_bias - rel_bias[N_BUCKETS - 1:N_BUCKETS, :]) * LOG2E
    return tbl[_rel_buckets(max_dist), :]


def _attn_sample_kernel(pt_ref, q_ref, kn_ref, vn_ref, bias_ref, lq_ref, sg_ref, *rest,
                        n_pages, t_new, lam_init):
    k_refs = rest[:n_pages]
    v_refs = rest[n_pages:2 * n_pages]
    o_ref, kb_s, vb_s = rest[2 * n_pages:]
    past = n_pages * PAGE_SIZE

    qf = q_ref[0]
    lane = lax.broadcasted_iota(jnp.int32, (t_new, V_DIM), 1)
    pad = jnp.zeros((PAGE_SIZE - t_new, V_DIM), F32)
    lam = _lambda(lq_ref, lam_init)

    for h in range(N_KV_HEADS):
        head_rows = pl.ds(h, PAGE_SIZE, stride=N_KV_HEADS)
        for j in range(n_pages):
            dst = pl.ds(j * PAGE_SIZE, PAGE_SIZE)
            kb_s[h, dst, :] = k_refs[j][0, 0, head_rows, :].astype(BF16)
            vb_s[h, dst, :] = v_refs[j][0, 0, head_rows, :].astype(BF16)
        new = pl.ds(past, PAGE_SIZE)
        hcols = slice(h * V_DIM, (h + 1) * V_DIM)
        kb_s[h, new, :] = jnp.concatenate([kn_ref[0][:, hcols], pad], axis=0).astype(BF16)
        vb_s[h, new, :] = jnp.concatenate([vn_ref[0][:, hcols], pad], axis=0).astype(BF16)

        pieces = []
        for mi in range(2):
            keep = (lane < HEAD_DIM) if mi == 0 else (lane >= HEAD_DIM)
            for g in range(GROUP):
                c0 = (h * GROUP + g) * V_DIM
                pieces.append(jnp.where(keep, qf[:, c0:c0 + V_DIM], 0.0))
        wq = jnp.concatenate(pieces, axis=0).astype(BF16)

        s = lax.dot_general(wq, kb_s[h], (((1,), (1,)), ((), ())),
                            preferred_element_type=F32) + bias_ref[h]
        m = jnp.max(s, axis=-1, keepdims=True)
        p = jnp.exp2(s - m)
        l = jnp.sum(p, axis=-1, keepdims=True)
        acc = jnp.dot(p.astype(BF16), vb_s[h], preferred_element_type=F32) / l
        for g in range(GROUP):
            o1 = acc[g * t_new:(g + 1) * t_new, :]
            o2 = acc[(GROUP + g) * t_new:(GROUP + g + 1) * t_new, :]
            c0 = (h * GROUP + g) * V_DIM
            o_ref[0, :, c0:c0 + V_DIM] = _head_norm(o1 - lam * o2, sg_ref[...], lam_init)


def _attn_sample(page_table, q, kn, vn, bias, lq, sg, cache_k, cache_v, *, layer, lam_init):
    n, t_new, _ = q.shape
    n_pages = page_table.shape[1]
    width = N_KV_HEADS * V_DIM
    cols = (n_pages + 1) * PAGE_SIZE
    seq = lambda c: pl.BlockSpec((1, t_new, c), lambda i, pt: (i, 0, 0))
    const = lambda shape: pl.BlockSpec(shape, lambda i, pt: (0,) * len(shape),
                                       pipeline_mode=pl.Buffered(1))

    def page_spec(j):
        return pl.BlockSpec((1, 1, PAGE_SIZE * N_KV_HEADS, V_DIM),
                            lambda i, pt: (layer, pt[i * n_pages + j], 0, 0))

    pages = [page_spec(j) for j in range(n_pages)]
    grid_spec = pltpu.PrefetchScalarGridSpec(
        num_scalar_prefetch=1, grid=(n,),
        in_specs=[seq(D_MODEL), seq(width), seq(width),
                  const((N_KV_HEADS, 2 * GROUP * t_new, cols)),
                  const((4, HEAD_DIM)), const((1, V_DIM))] + pages + pages,
        out_specs=seq(D_MODEL),
        scratch_shapes=[pltpu.VMEM((N_KV_HEADS, cols, V_DIM), BF16),
                        pltpu.VMEM((N_KV_HEADS, cols, V_DIM), BF16)])
    return pl.pallas_call(
        functools.partial(_attn_sample_kernel, n_pages=n_pages, t_new=t_new, lam_init=lam_init),
        grid_spec=grid_spec,
        out_shape=jax.ShapeDtypeStruct((n, t_new, D_MODEL), F32),
        compiler_params=_params(1),
        name="attn_sample",
    )(page_table.reshape(-1), q, kn, vn, bias, lq, sg,
      *([cache_k] * n_pages), *([cache_v] * n_pages))


def _sample_bias(rel_bias, n_pages, t_new):
    past = n_pages * PAGE_SIZE
    cols = past + PAGE_SIZE
    by_dist = _bias_by_distance(rel_bias, past + t_new - 1).T
    rows = []
    for qi in range(t_new):
        seen = by_dist[:, :past + qi + 1][:, ::-1]
        hidden = jnp.full((N_HEADS, cols - (past + qi + 1)), NEG_INF, F32)
        rows.append(jnp.concatenate([seen, hidden], axis=1))
    tile = jnp.stack(rows, axis=1)
    tile = tile.reshape(N_KV_HEADS, GROUP * t_new, cols)
    return jnp.concatenate([tile, tile], axis=1)


def _mixout_kernel(a_ref, cy_ref, sga_ref, sgc_ref, x_ref, wao_ref, wco_ref, wo_ref, o_ref):
    a = jnp.dot(a_ref[...].astype(BF16), wao_ref[...], preferred_element_type=F32)
    c = jnp.dot(cy_ref[...], wco_ref[...], preferred_element_type=F32)
    merged = sga_ref[...].astype(F32) * a + sgc_ref[...].astype(F32) * c
    o_ref[...] = x_ref[...] + jnp.dot(merged.astype(BF16), wo_ref[...],
                                      preferred_element_type=F32)


def _mixout(attn, cy, sga, sgc, x, wao, wco, wo, *, layer):
    t = x.shape[0]
    tm = min(ROW_TILE, t)
    row = pl.BlockSpec((tm, D_MODEL), lambda i: (i, 0))
    w = _layer_spec((D_MODEL, D_MODEL), layer)
    return pl.pallas_call(
        _mixout_kernel, grid=(t // tm,),
        in_specs=[row, row, row, row, row, w, w, w], out_specs=row,
        out_shape=jax.ShapeDtypeStruct((t, D_MODEL), F32),
        compiler_params=_params(1), name="mixout",
    )(attn, cy, sga, sgc, x, wao, wco, wo)


def _ffn_kernel(x_ref, g_ref, wgu_ref, wd_ref, o_ref):
    x = x_ref[...]
    ms = jnp.mean(x * x, axis=-1, keepdims=True)
    h = (x * lax.rsqrt(ms + EPS) * g_ref[...]).astype(BF16)
    y = None
    for a, b in FFN_CHUNKS:
        gate = jnp.dot(h, wgu_ref[:, a:b], preferred_element_type=F32)
        up = jnp.dot(h, wgu_ref[:, FFN_DIM + a:FFN_DIM + b], preferred_element_type=F32)
        act = (gate * jax.nn.sigmoid(gate) * up).astype(BF16)
        part = jnp.dot(act, wd_ref[a:b, :], preferred_element_type=F32)
        y = part if y is None else y + part
    o_ref[...] = x + y


def _ffn(x, g, wgu, wd, *, layer):
    t = x.shape[0]
    tm = min(ROW_TILE, t)
    row = pl.BlockSpec((tm, D_MODEL), lambda i: (i, 0))
    return pl.pallas_call(
        _ffn_kernel, grid=(t // tm,),
        in_specs=[row, _const_spec((1, D_MODEL)), _layer_spec((D_MODEL, 2 * FFN_DIM), layer),
                  _layer_spec((FFN_DIM, D_MODEL), layer)],
        out_specs=row, out_shape=jax.ShapeDtypeStruct((t, D_MODEL), F32),
        compiler_params=_params(1), name="ffn",
    )(x, g, wgu, wd)


def kernel(x_prompt, x_sample, cache_k, cache_v, state_conv, page_table, rel_bias, mix_norm_g, w_in, q_norm_g, k_norm_g, lambda_qk, subln_g, conv_w, w_attn_out, w_conv_out, w_out, ffn_norm_g, w_gate_up, w_down):
    batch, seq, _ = x_prompt.shape
    n_dec, t_new, _ = x_sample.shape
    depth = w_in.shape[0]
    n_pool = cache_k.shape[1]
    n_pages = page_table.shape[1]
    width = N_KV_HEADS * V_DIM

    w_in_b, w_ao_b, w_co_b, w_o_b, w_gu_b, w_d_b = (
        w.astype(BF16) for w in (w_in, w_attn_out, w_conv_out, w_out, w_gate_up, w_down))
    blk = np.arange(D_MODEL) // HEAD_DIM
    bd = jnp.asarray(blk[:, None] == blk[None, :], BF16)
    cache_k4 = cache_k.reshape(depth, n_pool, PAGE_SIZE * N_KV_HEADS, V_DIM)
    cache_v4 = cache_v.reshape(depth, n_pool, PAGE_SIZE * N_KV_HEADS, V_DIM)

    tb = min(ATTN_TILE, seq)
    bdiag, bsub = _prompt_bias(rel_bias, tb)
    sbias = _sample_bias(rel_bias, n_pages, t_new)

    first = jnp.zeros((depth, n_dec, t_new - 1, D_MODEL), F32)
    s1 = jnp.concatenate([state_conv[:, :, 1:2], first], axis=2)
    s2 = jnp.concatenate([state_conv, first[:, :, 1:]], axis=2)
    s1 = s1.reshape(depth, n_dec * t_new, D_MODEL)
    s2 = s2.reshape(depth, n_dec * t_new, D_MODEL)

    xp = x_prompt.reshape(batch * seq, D_MODEL)
    xs = x_sample.reshape(n_dec * t_new, D_MODEL)
    tiles_per_seq = max(seq // min(ROW_TILE, batch * seq), 1)
    conv_p, k_s, v_s, conv_s = [], [], [], []
    new_k = jnp.zeros((depth * batch * seq * N_KV_HEADS, V_DIM), F32)
    new_v = jnp.zeros((depth * batch * seq * N_KV_HEADS, V_DIM), F32)
    for l in range(depth):
        lam_init = 0.8 - 0.6 * math.exp(-0.3 * l)
        g_mix = mix_norm_g[l].reshape(1, D_MODEL)
        qg = jnp.tile(q_norm_g[l], D_MODEL // HEAD_DIM).reshape(1, D_MODEL)
        kg = jnp.tile(k_norm_g[l], width // HEAD_DIM).reshape(1, width)
        sg = subln_g[l].reshape(1, V_DIM)
        lq = lambda_qk[l]

        q, kz, vb, new_k, new_v, cy, sga, sgc, cu = _inproj(
            xp, g_mix, w_in_b, bd, qg, kg, conv_w[l], seq_len=seq,
            new_kv=(new_k, new_v), layer=l)
        attn = _attn_prompt(q.reshape(batch, seq, D_MODEL), kz.reshape(batch, seq, D_MODEL),
                            vb.reshape(batch, seq, width), bdiag, bsub, lq, sg,
                            lam_init=lam_init)
        xp = _mixout(attn.reshape(batch * seq, D_MODEL), cy, sga, sgc, xp,
                     w_ao_b, w_co_b, w_o_b, layer=l)
        xp = _ffn(xp, ffn_norm_g[l].reshape(1, D_MODEL), w_gu_b, w_d_b, layer=l)
        tails = cu.reshape(batch, tiles_per_seq, SUBLANES, D_MODEL)
        conv_p.append(tails[:, -1, SUBLANES - (CONV_K - 1):, :])

        q, k, v, cy, sga, sgc, cu = _inproj(xs, g_mix, w_in_b, bd, qg, kg, conv_w[l],
                                            seq_len=t_new, state=(s1[l], s2[l]), layer=l)
        attn = _attn_sample(page_table, q.reshape(n_dec, t_new, D_MODEL),
                            k.reshape(n_dec, t_new, width), v.reshape(n_dec, t_new, width),
                            sbias, lq, sg, cache_k4, cache_v4, layer=l, lam_init=lam_init)
        xs = _mixout(attn.reshape(n_dec * t_new, D_MODEL), cy, sga, sgc, xs,
                     w_ao_b, w_co_b, w_o_b, layer=l)
        xs = _ffn(xs, ffn_norm_g[l].reshape(1, D_MODEL), w_gu_b, w_d_b, layer=l)
        k_s.append(k.reshape(n_dec, t_new, N_KV_HEADS, V_DIM))
        v_s.append(v.reshape(n_dec, t_new, N_KV_HEADS, V_DIM))
        conv_s.append(cu.reshape(n_dec, t_new, D_MODEL)[:, t_new - (CONV_K - 1):, :])

    return (xp.reshape(batch, seq, D_MODEL), xs.reshape(n_dec, t_new, D_MODEL),
            new_k.reshape(depth, batch, seq, N_KV_HEADS, V_DIM),
            new_v.reshape(depth, batch, seq, N_KV_HEADS, V_DIM), jnp.stack(conv_p),
            jnp.stack(k_s), jnp.stack(v_s), jnp.stack(conv_s))
```

```python
import functools
import math

import jax
import jax.numpy as jnp
import numpy as np
from jax import lax
from jax.experimental import pallas as pl
from jax.experimental.pallas import tpu as pltpu

F32 = jnp.float32
BF16 = jnp.bfloat16

D_MODEL = 1024
N_HEADS = 8
N_KV_HEADS = 4
GROUP = N_HEADS // N_KV_HEADS
HEAD_DIM = 64
V_DIM = 2 * HEAD_DIM
ATTN_SCALE = HEAD_DIM ** -0.5
CONV_K = 3
FFN_DIM = 2816
N_BUCKETS = 32
MAX_EXACT = N_BUCKETS // 2
REL_MAX_DIST = 128
EPS = 1e-6
NEG_INF = -1e30
PAGE_SIZE = 128

OFF_Q, OFF_K, OFF_V, OFF_GB, OFF_GC, OFF_U, OFF_GA, OFF_GCV, IN_COLS = (
    0, 1024, 1536, 2048, 3072, 4096, 5120, 6144, 7168)

VMEM_LIMIT_BYTES = 56 * 1024 * 1024
SUBLANES = 8
LANES = 128
LOG2E = math.log2(math.e)
ROW_TILE = 512
ATTN_TILE = 512
FFN_CHUNKS = ((0, 1536), (1536, 2816))


def _params(n_axes, vmem=VMEM_LIMIT_BYTES):
    return pltpu.CompilerParams(
        dimension_semantics=("arbitrary",) * n_axes, vmem_limit_bytes=vmem)


def _const_spec(shape):
    zeros = (0,) * len(shape)
    return pl.BlockSpec(shape, lambda *_: zeros, pipeline_mode=pl.Buffered(1))


def _layer_spec(shape, layer):
    zeros = (0,) * len(shape)
    return pl.BlockSpec((None,) + tuple(shape), lambda *_: (layer,) + zeros,
                        pipeline_mode=pl.Buffered(1))


def _rel_buckets(max_dist):
    n = np.arange(max_dist + 1)
    nf = np.maximum(n, 1).astype(np.float32)
    large = MAX_EXACT + (np.log(nf / np.float32(MAX_EXACT)) / np.float32(math.log(REL_MAX_DIST / MAX_EXACT))
                         * np.float32(N_BUCKETS - MAX_EXACT)).astype(np.int32)
    large = np.minimum(large, N_BUCKETS - 1)
    return np.where(n < MAX_EXACT, n, large).astype(np.int32)


def _inproj_kernel(*refs, tm, tiles_per_seq, sample):
    if sample:
        (x_ref, g_ref, w_ref, bd_ref, qg_ref, kg_ref, cw_ref, s1_ref, s2_ref,
         q_ref, k_ref, v_ref, cy_ref, sga_ref, sgc_ref, cu_ref, ext) = refs
    else:
        (x_ref, g_ref, w_ref, bd_ref, qg_ref, kg_ref, cw_ref, _, _,
         q_ref, kz_ref, vb_ref, knew_ref, vnew_ref, cy_ref, sga_ref, sgc_ref, cu_ref,
         ext) = refs
    i = pl.program_id(0)

    x = x_ref[...]
    ms = jnp.mean(x * x, axis=-1, keepdims=True)
    h = (x * lax.rsqrt(ms + EPS) * g_ref[...]).astype(BF16)

    def proj(a, b):
        return jnp.dot(h, w_ref[:, a:b], preferred_element_type=F32)

    def group_norm(y, bd, gain):
        ss = jnp.dot((y * y).astype(BF16), bd, preferred_element_type=F32)
        return y * lax.rsqrt(ss * (1.0 / HEAD_DIM) + EPS) * gain

    yq = proj(OFF_Q, OFF_K)
    q_ref[...] = (group_norm(yq, bd_ref[...], qg_ref[...]) * (ATTN_SCALE * LOG2E)).astype(q_ref.dtype)
    k = group_norm(proj(OFF_K, OFF_V), bd_ref[0:512, 0:512], kg_ref[...])
    v = proj(OFF_V, OFF_GB)
    if sample:
        k_ref[...] = k
        v_ref[...] = v
    else:
        lane = lax.broadcasted_iota(jnp.int32, (tm, V_DIM), 1)
        for hd in range(N_KV_HEADS):
            kh = k[:, hd * V_DIM:(hd + 1) * V_DIM]
            vh = v[:, hd * V_DIM:(hd + 1) * V_DIM]
            head_rows = pl.ds(hd, tm, stride=N_KV_HEADS)
            knew_ref[head_rows, :] = kh
            vnew_ref[head_rows, :] = vh
            c0 = hd * 2 * V_DIM
            kz_ref[:, c0:c0 + V_DIM] = jnp.where(lane < HEAD_DIM, kh, 0.0).astype(BF16)
            kz_ref[:, c0 + V_DIM:c0 + 2 * V_DIM] = jnp.where(lane >= HEAD_DIM, kh, 0.0).astype(BF16)
            vb_ref[:, hd * V_DIM:(hd + 1) * V_DIM] = vh.astype(BF16)

    @pl.when(i % tiles_per_seq == 0)
    def _():
        ext[0:SUBLANES, :] = jnp.zeros((SUBLANES, D_MODEL), F32)

    cu = proj(OFF_GC, OFF_U) * proj(OFF_U, OFF_GA)
    ext[SUBLANES:SUBLANES + tm, :] = cu
    p1 = ext[SUBLANES - 1:SUBLANES - 1 + tm, :]
    p2 = ext[SUBLANES - 2:SUBLANES - 2 + tm, :]
    if sample:
        rmod = lax.broadcasted_iota(jnp.int32, (tm, D_MODEL), 0) & (SUBLANES - 1)
        p1 = jnp.where(rmod >= 1, p1, s1_ref[...])
        p2 = jnp.where(rmod >= 2, p2, s2_ref[...])
    cw = cw_ref[...]
    conv = cw[0:1, :] * p2 + cw[1:2, :] * p1 + cw[2:3, :] * cu
    cy_ref[...] = (proj(OFF_GB, OFF_GC) * conv).astype(BF16)
    if sample:
        cu_ref[...] = cu
    else:
        cu_ref[...] = cu[tm - SUBLANES:tm, :]
        ext[0:SUBLANES, :] = cu[tm - SUBLANES:tm, :]

    sga_ref[...] = jax.nn.sigmoid(proj(OFF_GA, OFF_GCV)).astype(BF16)
    sgc_ref[...] = jax.nn.sigmoid(proj(OFF_GCV, IN_COLS)).astype(BF16)


def _inproj(x, g, w_in, bd, qg, kg, cw, *, seq_len, state=None, new_kv=None, layer=0):
    t = x.shape[0]
    tm = min(ROW_TILE, t)
    n_tiles = t // tm
    sample = state is not None
    tiles_per_seq = max(seq_len // tm, 1)
    row = lambda c: pl.BlockSpec((tm, c), lambda i: (i, 0))
    in_specs = [row(D_MODEL), _layer_spec((1, D_MODEL), layer),
                _layer_spec((D_MODEL, IN_COLS), layer), _const_spec((D_MODEL, D_MODEL)),
                _layer_spec((1, D_MODEL), layer), _layer_spec((1, 512), layer),
                _layer_spec((CONV_K, D_MODEL), layer)]
    args = [x, g, w_in, bd, qg, kg, cw]
    act = lambda dt: jax.ShapeDtypeStruct((t, D_MODEL), dt)
    if sample:
        in_specs += [pl.BlockSpec((None, tm, D_MODEL), lambda i: (layer, i, 0))] * 2
        args += list(state)
        out_shape = [act(F32), jax.ShapeDtypeStruct((t, 512), F32),
                     jax.ShapeDtypeStruct((t, 512), F32), act(BF16), act(BF16), act(BF16),
                     act(F32)]
        out_specs = [row(D_MODEL), row(512), row(512), row(D_MODEL), row(D_MODEL),
                     row(D_MODEL), row(D_MODEL)]
        aliases = {}
    else:
        in_specs += [pl.BlockSpec(memory_space=pl.ANY)] * 2
        args += list(new_kv)
        new_spec = pl.BlockSpec((tm * N_KV_HEADS, V_DIM), lambda i: (layer * n_tiles + i, 0))
        new_shape = jax.ShapeDtypeStruct(new_kv[0].shape, F32)
        out_shape = [act(BF16), act(BF16), jax.ShapeDtypeStruct((t, 512), BF16),
                     new_shape, new_shape, act(BF16), act(BF16), act(BF16),
                     jax.ShapeDtypeStruct((n_tiles * SUBLANES, D_MODEL), F32)]
        out_specs = [row(D_MODEL), row(D_MODEL), row(512), new_spec, new_spec,
                     row(D_MODEL), row(D_MODEL), row(D_MODEL),
                     pl.BlockSpec((SUBLANES, D_MODEL), lambda i: (i, 0))]
        aliases = {7: 3, 8: 4}
    return pl.pallas_call(
        functools.partial(_inproj_kernel, tm=tm, tiles_per_seq=tiles_per_seq, sample=sample),
        grid=(n_tiles,), in_specs=in_specs, out_specs=out_specs, out_shape=out_shape,
        scratch_shapes=[pltpu.VMEM((tm + SUBLANES, D_MODEL), F32)],
        input_output_aliases=aliases,
        compiler_params=_params(1),
        name="inproj_sample" if sample else "inproj_prompt",
    )(*args)


def _lambda(lq_ref, lam_init):
    lq = lq_ref[...]
    a = jnp.sum(lq[0:1, :] * lq[1:2, :], axis=-1, keepdims=True)
    b = jnp.sum(lq[2:3, :] * lq[3:4, :], axis=-1, keepdims=True)
    return jnp.exp(a) - jnp.exp(b) + lam_init


def _head_norm(o, gain, lam_init):
    ms = jnp.mean(o * o, axis=-1, keepdims=True)
    return o * lax.rsqrt(ms + EPS) * gain * (1.0 - lam_init)


def _attn_prompt_kernel(q_ref, kz_ref, vb_ref, bdiag_ref, bsub_ref, lq_ref, sg_ref, o_ref,
                        qs_s, m_s, l_s, acc_s, *, tb, lam_init):
    qi = pl.program_id(2)

    for g in range(GROUP):
        qs_s[g * tb:(g + 1) * tb, :] = q_ref[0, :, g * V_DIM:(g + 1) * V_DIM]
    m_s[...] = jnp.full(m_s.shape, NEG_INF, F32)
    l_s[...] = jnp.zeros(l_s.shape, F32)
    acc_s[...] = jnp.zeros(acc_s.shape, F32)

    def block(start, width, bias):
        rows = pl.ds(pl.multiple_of(start, tb), width)
        q = qs_s[...]
        vb = vb_ref[0, rows, :]
        for mi in range(2):
            kz = kz_ref[0, rows, mi * V_DIM:(mi + 1) * V_DIM]
            s = lax.dot_general(q, kz, (((1,), (1,)), ((), ())),
                                preferred_element_type=F32)
            if bias is not None:
                s = s + bias
            cols = [s[:, c * LANES:(c + 1) * LANES] for c in range(width // LANES)]
            m_old = m_s[mi]
            m_new = jnp.maximum(m_old, jnp.max(functools.reduce(jnp.maximum, cols),
                                               axis=-1, keepdims=True))
            alpha = jnp.exp2(m_old - m_new)
            ps = [jnp.exp2(c - m_new) for c in cols]
            l_s[mi] = alpha * l_s[mi] + functools.reduce(jnp.add, ps)
            p = jnp.concatenate([x.astype(BF16) for x in ps], axis=1)
            acc_s[mi] = alpha * acc_s[mi] + jnp.dot(p, vb, preferred_element_type=F32)
            m_s[mi] = m_new

    n_far = jnp.maximum(qi - 1, 0)

    def far(j, carry):
        block(j * (2 * tb), 2 * tb, None)
        return carry

    lax.fori_loop(0, n_far // 2, far, 0)

    @pl.when(n_far % 2 == 1)
    def _():
        block((n_far - 1) * tb, tb, None)

    @pl.when(qi >= 1)
    def _():
        near = jnp.concatenate([bsub_ref[0], bdiag_ref[0]], axis=1)
        block((qi - 1) * tb, 2 * tb, near)

    @pl.when(qi == 0)
    def _():
        block(0, tb, bdiag_ref[0])

    lam = _lambda(lq_ref, lam_init)
    l1 = jnp.sum(l_s[0], axis=-1, keepdims=True)
    l2 = jnp.sum(l_s[1], axis=-1, keepdims=True)
    o = acc_s[0] / l1 - lam * (acc_s[1] / l2)
    o = _head_norm(o, sg_ref[...], lam_init)
    for g in range(GROUP):
        o_ref[0, :, g * V_DIM:(g + 1) * V_DIM] = o[g * tb:(g + 1) * tb, :].astype(o_ref.dtype)


def _attn_prompt(q, kz, vb, bdiag, bsub, lq, sg, *, layer, lam_init):
    b, s, _ = q.shape
    tb = min(ATTN_TILE, s)
    rows2 = GROUP * tb
    q_spec = pl.BlockSpec((1, tb, GROUP * V_DIM), lambda bi, h, qi: (bi, qi, h))
    kz_spec = pl.BlockSpec((1, s, 2 * V_DIM), lambda bi, h, qi: (bi, 0, h))
    vb_spec = pl.BlockSpec((1, s, V_DIM), lambda bi, h, qi: (bi, 0, h))
    bias_spec = pl.BlockSpec((1, rows2, tb), lambda bi, h, qi: (h, 0, 0))
    return pl.pallas_call(
        functools.partial(_attn_prompt_kernel, tb=tb, lam_init=lam_init),
        grid=(b, N_KV_HEADS, s // tb),
        in_specs=[q_spec, kz_spec, vb_spec, bias_spec, bias_spec,
                  _layer_spec((4, HEAD_DIM), layer), _layer_spec((1, V_DIM), layer)],
        out_specs=q_spec,
        out_shape=jax.ShapeDtypeStruct((b, s, D_MODEL), BF16),
        scratch_shapes=[pltpu.VMEM((rows2, V_DIM), BF16),
                        pltpu.VMEM((2, rows2, LANES), F32), pltpu.VMEM((2, rows2, LANES), F32),
                        pltpu.VMEM((2, rows2, V_DIM), F32)],
        compiler_params=_params(3),
        name="attn_prompt",
    )(q, kz, vb, bdiag, bsub, lq, sg)


def _prompt_bias(rel_bias, tb):
    by_dist = _bias_by_distance(rel_bias, 2 * tb - 1).T
    masked = jnp.full((N_HEADS, tb), NEG_INF, F32)

    def first_row(delta):
        if delta == 0:
            left = jnp.concatenate([by_dist[:, 0:1], masked[:, 1:]], axis=1)
        else:
            left = by_dist[:, delta:delta - tb:-1]
        right = by_dist[:, delta + 1:delta + tb][:, ::-1]
        return jnp.concatenate([left, masked[:, 0:1], right], axis=1)

    rows = jnp.stack([first_row(0), first_row(tb)], axis=1)
    tile = jax.ShapeDtypeStruct((N_HEADS, tb, tb), F32)
    tile_spec = pl.BlockSpec((1, tb, tb), lambda h: (h, 0, 0))
    bdiag, bsub = pl.pallas_call(
        functools.partial(_toeplitz_kernel, tb=tb), grid=(N_HEADS,),
        in_specs=[pl.BlockSpec((1, 2, 2 * tb), lambda h: (h, 0, 0))],
        out_specs=[tile_spec, tile_spec], out_shape=[tile, tile],
        compiler_params=_params(1), name="bias_tiles",
    )(rows)
    shape = (N_KV_HEADS, GROUP * tb, tb)
    return bdiag.reshape(shape), bsub.reshape(shape)


def _toeplitz_kernel(v_ref, diag_ref, sub_ref, *, tb):
    for t, o_ref in enumerate((diag_ref, sub_ref)):
        x = jnp.broadcast_to(v_ref[0, t:t + 1, :], (tb, 2 * tb))
        o_ref[0] = pltpu.roll(x, 0, 1, stride=1, stride_axis=0)[:, :tb]


def _bias_by_distance(rel_bias, max_dist):
    tbl = (rel_bias - rel_bias[N_BUCKETS - 1:N_BUCKETS, :]) * LOG2E
    return tbl[_rel_buckets(max_dist), :]


def _attn_sample_kernel(pt_ref, q_ref, kn_ref, vn_ref, bias_ref, lq_ref, sg_ref, ck_hbm, cv_hbm,
                        o_ref, kpg, vpg, sems, kb_s, vb_s, *, n_pages, t_new, layer, lam_init):
    i = pl.program_id(0)
    slot = i % 2
    past = n_pages * PAGE_SIZE

    def page_copies(seq, buf):
        copies = []
        for j in range(n_pages):
            page = pt_ref[seq * n_pages + j]
            copies.append(pltpu.make_async_copy(
                ck_hbm.at[layer, page], kpg.at[buf, j], sems.at[buf, 0, j]))
            copies.append(pltpu.make_async_copy(
                cv_hbm.at[layer, page], vpg.at[buf, j], sems.at[buf, 1, j]))
        return copies

    @pl.when(i == 0)
    def _():
        for c in page_copies(0, 0):
            c.start()

    @pl.when(i + 1 < pl.num_programs(0))
    def _():
        for c in page_copies(i + 1, 1 - slot):
            c.start()

    for c in page_copies(i, slot):
        c.wait()

    qf = q_ref[0]
    lane = lax.broadcasted_iota(jnp.int32, (t_new, V_DIM), 1)
    pad = jnp.zeros((PAGE_SIZE - t_new, V_DIM), F32)
    lam = _lambda(lq_ref, lam_init)

    for h in range(N_KV_HEADS):
        head_rows = pl.ds(h, PAGE_SIZE, stride=N_KV_HEADS)
        for j in range(n_pages):
            dst = pl.ds(j * PAGE_SIZE, PAGE_SIZE)
            kb_s[h, dst, :] = kpg[slot, j, head_rows, :].astype(BF16)
            vb_s[h, dst, :] = vpg[slot, j, head_rows, :].astype(BF16)
        new = pl.ds(past, PAGE_SIZE)
        hcols = slice(h * V_DIM, (h + 1) * V_DIM)
        kb_s[h, new, :] = jnp.concatenate([kn_ref[0][:, hcols], pad], axis=0).astype(BF16)
        vb_s[h, new, :] = jnp.concatenate([vn_ref[0][:, hcols], pad], axis=0).astype(BF16)

        pieces = []
        for mi in range(2):
            keep = (lane < HEAD_DIM) if mi == 0 else (lane >= HEAD_DIM)
            for g in range(GROUP):
                c0 = (h * GROUP + g) * V_DIM
                pieces.append(jnp.where(keep, qf[:, c0:c0 + V_DIM], 0.0))
        wq = jnp.concatenate(pieces, axis=0).astype(BF16)

        s = lax.dot_general(wq, kb_s[h], (((1,), (1,)), ((), ())),
                            preferred_element_type=F32) + bias_ref[h]
        m = jnp.max(s, axis=-1, keepdims=True)
        p = jnp.exp2(s - m)
        l = jnp.sum(p, axis=-1, keepdims=True)
        acc = jnp.dot(p.astype(BF16), vb_s[h], preferred_element_type=F32) / l
        for g in range(GROUP):
            o1 = acc[g * t_new:(g + 1) * t_new, :]
            o2 = acc[(GROUP + g) * t_new:(GROUP + g + 1) * t_new, :]
            c0 = (h * GROUP + g) * V_DIM
            o_ref[0, :, c0:c0 + V_DIM] = _head_norm(o1 - lam * o2, sg_ref[...], lam_init)


def _attn_sample(page_table, q, kn, vn, bias, lq, sg, cache_k, cache_v, *, layer, lam_init):
    n, t_new, _ = q.shape
    n_pages = page_table.shape[1]
    width = N_KV_HEADS * V_DIM
    cols = (n_pages + 1) * PAGE_SIZE
    seq = lambda c: pl.BlockSpec((1, t_new, c), lambda i, pt: (i, 0, 0))
    const = lambda shape: pl.BlockSpec(shape, lambda i, pt: (0,) * len(shape),
                                       pipeline_mode=pl.Buffered(1))

    page_rows = PAGE_SIZE * N_KV_HEADS
    grid_spec = pltpu.PrefetchScalarGridSpec(
        num_scalar_prefetch=1, grid=(n,),
        in_specs=[seq(D_MODEL), seq(width), seq(width),
                  const((N_KV_HEADS, 2 * GROUP * t_new, cols)),
                  _layer_spec((4, HEAD_DIM), layer), _layer_spec((1, V_DIM), layer),
                  pl.BlockSpec(memory_space=pl.ANY), pl.BlockSpec(memory_space=pl.ANY)],
        out_specs=seq(D_MODEL),
        scratch_shapes=[pltpu.VMEM((2, n_pages, page_rows, V_DIM), F32),
                        pltpu.VMEM((2, n_pages, page_rows, V_DIM), F32),
                        pltpu.SemaphoreType.DMA((2, 2, n_pages)),
                        pltpu.VMEM((N_KV_HEADS, cols, V_DIM), BF16),
                        pltpu.VMEM((N_KV_HEADS, cols, V_DIM), BF16)])
    return pl.pallas_call(
        functools.partial(_attn_sample_kernel, n_pages=n_pages, t_new=t_new, layer=layer,
                          lam_init=lam_init),
        grid_spec=grid_spec,
        out_shape=jax.ShapeDtypeStruct((n, t_new, D_MODEL), F32),
        compiler_params=_params(1),
        name="attn_sample",
    )(page_table.reshape(-1), q, kn, vn, bias, lq, sg, cache_k, cache_v)


def _sample_bias(rel_bias, n_pages, t_new):
    past = n_pages * PAGE_SIZE
    cols = past + PAGE_SIZE
    by_dist = _bias_by_distance(rel_bias, past + t_new - 1).T
    rows = []
    for qi in range(t_new):
        seen = by_dist[:, :past + qi + 1][:, ::-1]
        hidden = jnp.full((N_HEADS, cols - (past + qi + 1)), NEG_INF, F32)
        rows.append(jnp.concatenate([seen, hidden], axis=1))
    tile = jnp.stack(rows, axis=1)
    tile = tile.reshape(N_KV_HEADS, GROUP * t_new, cols)
    return jnp.concatenate([tile, tile], axis=1)


def _mixout_kernel(a_ref, cy_ref, sga_ref, sgc_ref, x_ref, wao_ref, wco_ref, wo_ref, o_ref):
    a = jnp.dot(a_ref[...].astype(BF16), wao_ref[...], preferred_element_type=F32)
    c = jnp.dot(cy_ref[...], wco_ref[...], preferred_element_type=F32)
    merged = sga_ref[...].astype(F32) * a + sgc_ref[...].astype(F32) * c
    o_ref[...] = x_ref[...] + jnp.dot(merged.astype(BF16), wo_ref[...],
                                      preferred_element_type=F32)


def _mixout(attn, cy, sga, sgc, x, wao, wco, wo, *, layer):
    t = x.shape[0]
    tm = min(ROW_TILE, t)
    row = pl.BlockSpec((tm, D_MODEL), lambda i: (i, 0))
    w = _layer_spec((D_MODEL, D_MODEL), layer)
    return pl.pallas_call(
        _mixout_kernel, grid=(t // tm,),
        in_specs=[row, row, row, row, row, w, w, w], out_specs=row,
        out_shape=jax.ShapeDtypeStruct((t, D_MODEL), F32),
        compiler_params=_params(1), name="mixout",
    )(attn, cy, sga, sgc, x, wao, wco, wo)


def _ffn_kernel(x_ref, g_ref, wgu_ref, wd_ref, o_ref):
    x = x_ref[...]
    ms = jnp.mean(x * x, axis=-1, keepdims=True)
    h = (x * lax.rsqrt(ms + EPS) * g_ref[...]).astype(BF16)
    y = None
    for a, b in FFN_CHUNKS:
        gate = jnp.dot(h, wgu_ref[:, a:b], preferred_element_type=F32)
        up = jnp.dot(h, wgu_ref[:, FFN_DIM + a:FFN_DIM + b], preferred_element_type=F32)
        act = (gate * jax.nn.sigmoid(gate) * up).astype(BF16)
        part = jnp.dot(act, wd_ref[a:b, :], preferred_element_type=F32)
        y = part if y is None else y + part
    o_ref[...] = x + y


def _ffn(x, g, wgu, wd, *, layer):
    t = x.shape[0]
    tm = min(ROW_TILE, t)
    row = pl.BlockSpec((tm, D_MODEL), lambda i: (i, 0))
    return pl.pallas_call(
        _ffn_kernel, grid=(t // tm,),
        in_specs=[row, _layer_spec((1, D_MODEL), layer),
                  _layer_spec((D_MODEL, 2 * FFN_DIM), layer),
                  _layer_spec((FFN_DIM, D_MODEL), layer)],
        out_specs=row, out_shape=jax.ShapeDtypeStruct((t, D_MODEL), F32),
        compiler_params=_params(1), name="ffn",
    )(x, g, wgu, wd)


def kernel(x_prompt, x_sample, cache_k, cache_v, state_conv, page_table, rel_bias, mix_norm_g, w_in, q_norm_g, k_norm_g, lambda_qk, subln_g, conv_w, w_attn_out, w_conv_out, w_out, ffn_norm_g, w_gate_up, w_down):
    batch, seq, _ = x_prompt.shape
    n_dec, t_new, _ = x_sample.shape
    depth = w_in.shape[0]
    n_pool = cache_k.shape[1]
    n_pages = page_table.shape[1]
    width = N_KV_HEADS * V_DIM

    w_in_b, w_ao_b, w_co_b, w_o_b, w_gu_b, w_d_b = (
        w.astype(BF16) for w in (w_in, w_attn_out, w_conv_out, w_out, w_gate_up, w_down))
    blk = np.arange(D_MODEL) // HEAD_DIM
    bd = jnp.asarray(blk[:, None] == blk[None, :], BF16)
    cache_k4 = cache_k.reshape(depth, n_pool, PAGE_SIZE * N_KV_HEADS, V_DIM)
    cache_v4 = cache_v.reshape(depth, n_pool, PAGE_SIZE * N_KV_HEADS, V_DIM)

    tb = min(ATTN_TILE, seq)
    bdiag, bsub = _prompt_bias(rel_bias, tb)
    sbias = _sample_bias(rel_bias, n_pages, t_new)

    first = jnp.zeros((depth, n_dec, t_new - 1, D_MODEL), F32)
    s1 = jnp.concatenate([state_conv[:, :, 1:2], first], axis=2)
    s2 = jnp.concatenate([state_conv, first[:, :, 1:]], axis=2)
    s1 = s1.reshape(depth, n_dec * t_new, D_MODEL)
    s2 = s2.reshape(depth, n_dec * t_new, D_MODEL)

    xp = x_prompt.reshape(batch * seq, D_MODEL)
    xs = x_sample.reshape(n_dec * t_new, D_MODEL)
    tiles_per_seq = max(seq // min(ROW_TILE, batch * seq), 1)
    conv_p, k_s, v_s, conv_s = [], [], [], []
    new_k = jnp.zeros((depth * batch * seq * N_KV_HEADS, V_DIM), F32)
    new_v = jnp.zeros((depth * batch * seq * N_KV_HEADS, V_DIM), F32)
    g_mix = mix_norm_g.reshape(depth, 1, D_MODEL)
    g_ffn = ffn_norm_g.reshape(depth, 1, D_MODEL)
    qg = jnp.tile(q_norm_g, (1, D_MODEL // HEAD_DIM)).reshape(depth, 1, D_MODEL)
    kg = jnp.tile(k_norm_g, (1, width // HEAD_DIM)).reshape(depth, 1, width)
    sg = subln_g.reshape(depth, 1, V_DIM)
    lq = lambda_qk
    for l in range(depth):
        lam_init = 0.8 - 0.6 * math.exp(-0.3 * l)

        q, kz, vb, new_k, new_v, cy, sga, sgc, cu = _inproj(
            xp, g_mix, w_in_b, bd, qg, kg, conv_w, seq_len=seq,
            new_kv=(new_k, new_v), layer=l)
        attn = _attn_prompt(q.reshape(batch, seq, D_MODEL), kz.reshape(batch, seq, D_MODEL),
                            vb.reshape(batch, seq, width), bdiag, bsub, lq, sg,
                            layer=l, lam_init=lam_init)
        xp = _mixout(attn.reshape(batch * seq, D_MODEL), cy, sga, sgc, xp,
                     w_ao_b, w_co_b, w_o_b, layer=l)
        xp = _ffn(xp, g_ffn, w_gu_b, w_d_b, layer=l)
        tails = cu.reshape(batch, tiles_per_seq, SUBLANES, D_MODEL)
        conv_p.append(tails[:, -1, SUBLANES - (CONV_K - 1):, :])

        q, k, v, cy, sga, sgc, cu = _inproj(xs, g_mix, w_in_b, bd, qg, kg, conv_w,
                                            seq_len=t_new, state=(s1, s2), layer=l)
        attn = _attn_sample(page_table, q.reshape(n_dec, t_new, D_MODEL),
                            k.reshape(n_dec, t_new, width), v.reshape(n_dec, t_new, width),
                            sbias, lq, sg, cache_k4, cache_v4, layer=l, lam_init=lam_init)
        xs = _mixout(attn.reshape(n_dec * t_new, D_MODEL), cy, sga, sgc, xs,
                     w_ao_b, w_co_b, w_o_b, layer=l)
        xs = _ffn(xs, g_ffn, w_gu_b, w_d_b, layer=l)
        k_s.append(k.reshape(n_dec, t_new, N_KV_HEADS, V_DIM))
        v_s.append(v.reshape(n_dec, t_new, N_KV_HEADS, V_DIM))
        conv_s.append(cu.reshape(n_dec, t_new, D_MODEL)[:, t_new - (CONV_K - 1):, :])

    return (xp.reshape(batch, seq, D_MODEL), xs.reshape(n_dec, t_new, D_MODEL),
            new_k.reshape(depth, batch, seq, N_KV_HEADS, V_DIM),
            new_v.reshape(depth, batch, seq, N_KV_HEADS, V_DIM), jnp.stack(conv_p),
            jnp.stack(k_s), jnp.stack(v_s), jnp.stack(conv_s))
```

```python
import functools
import math

import jax
import jax.numpy as jnp
import numpy as np
from jax import lax
from jax.experimental import pallas as pl
from jax.experimental.pallas import tpu as pltpu

F32 = jnp.float32
BF16 = jnp.bfloat16

D_MODEL = 1024
N_HEADS = 8
N_KV_HEADS = 4
GROUP = N_HEADS // N_KV_HEADS
HEAD_DIM = 64
V_DIM = 2 * HEAD_DIM
ATTN_SCALE = HEAD_DIM ** -0.5
CONV_K = 3
FFN_DIM = 2816
N_BUCKETS = 32
MAX_EXACT = N_BUCKETS // 2
REL_MAX_DIST = 128
EPS = 1e-6
NEG_INF = -1e30
PAGE_SIZE = 128

OFF_Q, OFF_K, OFF_V, OFF_GB, OFF_GC, OFF_U, OFF_GA, OFF_GCV, IN_COLS = (
    0, 1024, 1536, 2048, 3072, 4096, 5120, 6144, 7168)

VMEM_LIMIT_BYTES = 56 * 1024 * 1024
SUBLANES = 8
LANES = 128
LOG2E = math.log2(math.e)
ROW_TILE = 512
ATTN_TILE = 512
FFN_CHUNKS = ((0, 1536), (1536, 2816))


def _params(n_axes, vmem=VMEM_LIMIT_BYTES):
    return pltpu.CompilerParams(
        dimension_semantics=("arbitrary",) * n_axes, vmem_limit_bytes=vmem)


def _const_spec(shape):
    zeros = (0,) * len(shape)
    return pl.BlockSpec(shape, lambda *_: zeros, pipeline_mode=pl.Buffered(1))


def _layer_spec(shape, layer):
    zeros = (0,) * len(shape)
    return pl.BlockSpec((None,) + tuple(shape), lambda *_: (layer,) + zeros,
                        pipeline_mode=pl.Buffered(1))


def _rel_buckets(max_dist):
    n = np.arange(max_dist + 1)
    nf = np.maximum(n, 1).astype(np.float32)
    large = MAX_EXACT + (np.log(nf / np.float32(MAX_EXACT)) / np.float32(math.log(REL_MAX_DIST / MAX_EXACT))
                         * np.float32(N_BUCKETS - MAX_EXACT)).astype(np.int32)
    large = np.minimum(large, N_BUCKETS - 1)
    return np.where(n < MAX_EXACT, n, large).astype(np.int32)


def _inproj_kernel(*refs, tm, tiles_per_seq, sample):
    if sample:
        (x_ref, g_ref, w_ref, bd_ref, qg_ref, kg_ref, cw_ref, s1_ref, s2_ref,
         q_ref, k_ref, v_ref, cy_ref, sga_ref, sgc_ref, cu_ref, ext) = refs
    else:
        (x_ref, g_ref, w_ref, bd_ref, qg_ref, kg_ref, cw_ref, _, _,
         q_ref, kz_ref, vb_ref, knew_ref, vnew_ref, cy_ref, sga_ref, sgc_ref, cu_ref,
         ext) = refs
    i = pl.program_id(0)

    x = x_ref[...]
    ms = jnp.mean(x * x, axis=-1, keepdims=True)
    h = (x * lax.rsqrt(ms + EPS) * g_ref[...]).astype(BF16)

    def proj(a, b):
        return jnp.dot(h, w_ref[:, a:b], preferred_element_type=F32)

    def group_norm(y, bd, gain):
        ss = jnp.dot((y * y).astype(BF16), bd, preferred_element_type=F32)
        return y * lax.rsqrt(ss * (1.0 / HEAD_DIM) + EPS) * gain

    yq = proj(OFF_Q, OFF_K)
    q_ref[...] = (group_norm(yq, bd_ref[...], qg_ref[...]) * (ATTN_SCALE * LOG2E)).astype(q_ref.dtype)
    k = group_norm(proj(OFF_K, OFF_V), bd_ref[0:512, 0:512], kg_ref[...])
    v = proj(OFF_V, OFF_GB)
    if sample:
        k_ref[...] = k
        v_ref[...] = v
    else:
        lane = lax.broadcasted_iota(jnp.int32, (tm, V_DIM), 1)
        for hd in range(N_KV_HEADS):
            kh = k[:, hd * V_DIM:(hd + 1) * V_DIM]
            vh = v[:, hd * V_DIM:(hd + 1) * V_DIM]
            head_rows = pl.ds(hd, tm, stride=N_KV_HEADS)
            knew_ref[head_rows, :] = kh
            vnew_ref[head_rows, :] = vh
            c0 = hd * 2 * V_DIM
            kz_ref[:, c0:c0 + V_DIM] = jnp.where(lane < HEAD_DIM, kh, 0.0).astype(BF16)
            kz_ref[:, c0 + V_DIM:c0 + 2 * V_DIM] = jnp.where(lane >= HEAD_DIM, kh, 0.0).astype(BF16)
            vb_ref[:, hd * V_DIM:(hd + 1) * V_DIM] = vh.astype(BF16)

    @pl.when(i % tiles_per_seq == 0)
    def _():
        ext[0:SUBLANES, :] = jnp.zeros((SUBLANES, D_MODEL), F32)

    cu = proj(OFF_GC, OFF_U) * proj(OFF_U, OFF_GA)
    ext[SUBLANES:SUBLANES + tm, :] = cu
    p1 = ext[SUBLANES - 1:SUBLANES - 1 + tm, :]
    p2 = ext[SUBLANES - 2:SUBLANES - 2 + tm, :]
    if sample:
        rmod = lax.broadcasted_iota(jnp.int32, (tm, D_MODEL), 0) & (SUBLANES - 1)
        p1 = jnp.where(rmod >= 1, p1, s1_ref[...])
        p2 = jnp.where(rmod >= 2, p2, s2_ref[...])
    cw = cw_ref[...]
    conv = cw[0:1, :] * p2 + cw[1:2, :] * p1 + cw[2:3, :] * cu
    cy_ref[...] = (proj(OFF_GB, OFF_GC) * conv).astype(BF16)
    if sample:
        cu_ref[...] = cu
    else:
        cu_ref[...] = cu[tm - SUBLANES:tm, :]
        ext[0:SUBLANES, :] = cu[tm - SUBLANES:tm, :]

    sga_ref[...] = jax.nn.sigmoid(proj(OFF_GA, OFF_GCV)).astype(BF16)
    sgc_ref[...] = jax.nn.sigmoid(proj(OFF_GCV, IN_COLS)).astype(BF16)


def _inproj(x, g, w_in, bd, qg, kg, cw, *, seq_len, state=None, new_kv=None, layer=0):
    t = x.shape[0]
    tm = min(ROW_TILE, t)
    n_tiles = t // tm
    sample = state is not None
    tiles_per_seq = max(seq_len // tm, 1)
    row = lambda c: pl.BlockSpec((tm, c), lambda i: (i, 0))
    in_specs = [row(D_MODEL), _layer_spec((1, D_MODEL), layer),
                _layer_spec((D_MODEL, IN_COLS), layer), _const_spec((D_MODEL, D_MODEL)),
                _layer_spec((1, D_MODEL), layer), _layer_spec((1, 512), layer),
                _layer_spec((CONV_K, D_MODEL), layer)]
    args = [x, g, w_in, bd, qg, kg, cw]
    act = lambda dt: jax.ShapeDtypeStruct((t, D_MODEL), dt)
    if sample:
        in_specs += [pl.BlockSpec((None, tm, D_MODEL), lambda i: (layer, i, 0))] * 2
        args += list(state)
        out_shape = [act(F32), jax.ShapeDtypeStruct((t, 512), F32),
                     jax.ShapeDtypeStruct((t, 512), F32), act(BF16), act(BF16), act(BF16),
                     act(F32)]
        out_specs = [row(D_MODEL), row(512), row(512), row(D_MODEL), row(D_MODEL),
                     row(D_MODEL), row(D_MODEL)]
        aliases = {}
    else:
        in_specs += [pl.BlockSpec(memory_space=pl.ANY)] * 2
        args += list(new_kv)
        new_spec = pl.BlockSpec((tm * N_KV_HEADS, V_DIM), lambda i: (layer * n_tiles + i, 0))
        new_shape = jax.ShapeDtypeStruct(new_kv[0].shape, F32)
        out_shape = [act(BF16), act(BF16), jax.ShapeDtypeStruct((t, 512), BF16),
                     new_shape, new_shape, act(BF16), act(BF16), act(BF16),
                     jax.ShapeDtypeStruct((n_tiles * SUBLANES, D_MODEL), F32)]
        out_specs = [row(D_MODEL), row(D_MODEL), row(512), new_spec, new_spec,
                     row(D_MODEL), row(D_MODEL), row(D_MODEL),
                     pl.BlockSpec((SUBLANES, D_MODEL), lambda i: (i, 0))]
        aliases = {7: 3, 8: 4}
    return pl.pallas_call(
        functools.partial(_inproj_kernel, tm=tm, tiles_per_seq=tiles_per_seq, sample=sample),
        grid=(n_tiles,), in_specs=in_specs, out_specs=out_specs, out_shape=out_shape,
        scratch_shapes=[pltpu.VMEM((tm + SUBLANES, D_MODEL), F32)],
        input_output_aliases=aliases,
        compiler_params=_params(1),
        name="inproj_sample" if sample else "inproj_prompt",
    )(*args)


def _lambda(lq_ref, lam_init):
    lq = lq_ref[...]
    a = jnp.sum(lq[0:1, :] * lq[1:2, :], axis=-1, keepdims=True)
    b = jnp.sum(lq[2:3, :] * lq[3:4, :], axis=-1, keepdims=True)
    return jnp.exp(a) - jnp.exp(b) + lam_init


def _head_norm(o, gain, lam_init):
    ms = jnp.mean(o * o, axis=-1, keepdims=True)
    return o * lax.rsqrt(ms + EPS) * gain * (1.0 - lam_init)


def _attn_kernel(pt_ref, q_ref, kz_ref, vb_ref, bdiag_ref, bsub_ref, lq_ref, sg_ref,
                 qn_ref, kn_ref, vn_ref, sbias_ref, ck_hbm, cv_hbm, o_ref, on_ref,
                 qs_s, m_s, l_s, acc_s, kpg, vpg, sems, kb_s, vb_s,
                 *, tb, n_pages, t_new, layer, lam_init):
    qi = pl.program_id(2)
    n_q = pl.num_programs(2)
    seq = (pl.program_id(0) * N_KV_HEADS + pl.program_id(1)) * n_q + qi
    n_seq = pl.num_programs(0) * N_KV_HEADS * n_q
    slot = seq % 2
    past = n_pages * PAGE_SIZE

    def page_copies(s, buf):
        copies = []
        for j in range(n_pages):
            page = pt_ref[s * n_pages + j]
            copies.append(pltpu.make_async_copy(
                ck_hbm.at[layer, page], kpg.at[buf, j], sems.at[buf, 0, j]))
            copies.append(pltpu.make_async_copy(
                cv_hbm.at[layer, page], vpg.at[buf, j], sems.at[buf, 1, j]))
        return copies

    @pl.when(seq == 0)
    def _():
        for c in page_copies(0, 0):
            c.start()

    @pl.when(seq + 1 < n_seq)
    def _():
        for c in page_copies(seq + 1, 1 - slot):
            c.start()

    for c in page_copies(seq, slot):
        c.wait()

    for g in range(GROUP):
        qs_s[g * tb:(g + 1) * tb, :] = q_ref[0, :, g * V_DIM:(g + 1) * V_DIM]
    m_s[...] = jnp.full(m_s.shape, NEG_INF, F32)
    l_s[...] = jnp.zeros(l_s.shape, F32)
    acc_s[...] = jnp.zeros(acc_s.shape, F32)

    def block(start, width, bias):
        rows = pl.ds(pl.multiple_of(start, tb), width)
        q = qs_s[...]
        vb = vb_ref[0, rows, :]
        for mi in range(2):
            kz = kz_ref[0, rows, mi * V_DIM:(mi + 1) * V_DIM]
            s = lax.dot_general(q, kz, (((1,), (1,)), ((), ())),
                                preferred_element_type=F32)
            if bias is not None:
                s = s + bias
            cols = [s[:, c * LANES:(c + 1) * LANES] for c in range(width // LANES)]
            m_old = m_s[mi]
            m_new = jnp.maximum(m_old, jnp.max(functools.reduce(jnp.maximum, cols),
                                               axis=-1, keepdims=True))
            alpha = jnp.exp2(m_old - m_new)
            ps = [jnp.exp2(c - m_new) for c in cols]
            l_s[mi] = alpha * l_s[mi] + functools.reduce(jnp.add, ps)
            p = jnp.concatenate([x.astype(BF16) for x in ps], axis=1)
            acc_s[mi] = alpha * acc_s[mi] + jnp.dot(p, vb, preferred_element_type=F32)
            m_s[mi] = m_new

    n_far = jnp.maximum(qi - 1, 0)

    def far(j, carry):
        block(j * (2 * tb), 2 * tb, None)
        return carry

    lax.fori_loop(0, n_far // 2, far, 0)

    @pl.when(n_far % 2 == 1)
    def _():
        block((n_far - 1) * tb, tb, None)

    @pl.when(qi >= 1)
    def _():
        block((qi - 1) * tb, tb, bsub_ref[0])

    lam = _lambda(lq_ref, lam_init)

    qf = qn_ref[0]
    lane = lax.broadcasted_iota(jnp.int32, (t_new, V_DIM), 1)
    pad = jnp.zeros((PAGE_SIZE - t_new, V_DIM), F32)
    for h in range(N_KV_HEADS):
        head_rows = pl.ds(h, PAGE_SIZE, stride=N_KV_HEADS)
        for j in range(n_pages):
            dst = pl.ds(j * PAGE_SIZE, PAGE_SIZE)
            kb_s[h, dst, :] = kpg[slot, j, head_rows, :].astype(BF16)
            vb_s[h, dst, :] = vpg[slot, j, head_rows, :].astype(BF16)
        new = pl.ds(past, PAGE_SIZE)
        hcols = slice(h * V_DIM, (h + 1) * V_DIM)
        kb_s[h, new, :] = jnp.concatenate([kn_ref[0][:, hcols], pad], axis=0).astype(BF16)
        vb_s[h, new, :] = jnp.concatenate([vn_ref[0][:, hcols], pad], axis=0).astype(BF16)

        pieces = []
        for mi in range(2):
            keep = (lane < HEAD_DIM) if mi == 0 else (lane >= HEAD_DIM)
            for g in range(GROUP):
                c0 = (h * GROUP + g) * V_DIM
                pieces.append(jnp.where(keep, qf[:, c0:c0 + V_DIM], 0.0))
        wq = jnp.concatenate(pieces, axis=0).astype(BF16)

        s = lax.dot_general(wq, kb_s[h], (((1,), (1,)), ((), ())),
                            preferred_element_type=F32) + sbias_ref[h]
        m = jnp.max(s, axis=-1, keepdims=True)
        p = jnp.exp2(s - m)
        l = jnp.sum(p, axis=-1, keepdims=True)
        acc = jnp.dot(p.astype(BF16), vb_s[h], preferred_element_type=F32) / l
        for g in range(GROUP):
            o1 = acc[g * t_new:(g + 1) * t_new, :]
            o2 = acc[(GROUP + g) * t_new:(GROUP + g + 1) * t_new, :]
            c0 = (h * GROUP + g) * V_DIM
            on_ref[0, :, c0:c0 + V_DIM] = _head_norm(o1 - lam * o2, sg_ref[...], lam_init)

    block(qi * tb, tb, bdiag_ref[0])

    l1 = jnp.sum(l_s[0], axis=-1, keepdims=True)
    l2 = jnp.sum(l_s[1], axis=-1, keepdims=True)
    o = acc_s[0] / l1 - lam * (acc_s[1] / l2)
    o = _head_norm(o, sg_ref[...], lam_init)
    for g in range(GROUP):
        o_ref[0, :, g * V_DIM:(g + 1) * V_DIM] = o[g * tb:(g + 1) * tb, :].astype(o_ref.dtype)


def _attention(page_table, q, kz, vb, bdiag, bsub, lq, sg, qn, kn, vn, sbias, cache_k, cache_v,
               *, layer, lam_init):
    b, s, _ = q.shape
    n, t_new, _ = qn.shape
    tb = min(ATTN_TILE, s)
    n_q = s // tb
    assert n == b * N_KV_HEADS * n_q, "one sample sequence per attention grid step"
    n_pages = page_table.shape[1]
    width = N_KV_HEADS * V_DIM
    cols = (n_pages + 1) * PAGE_SIZE
    rows2 = GROUP * tb
    page_rows = PAGE_SIZE * N_KV_HEADS

    q_spec = pl.BlockSpec((1, tb, GROUP * V_DIM), lambda bi, h, qi, pt: (bi, qi, h))
    kz_spec = pl.BlockSpec((1, s, 2 * V_DIM), lambda bi, h, qi, pt: (bi, 0, h))
    vb_spec = pl.BlockSpec((1, s, V_DIM), lambda bi, h, qi, pt: (bi, 0, h))
    bias_spec = pl.BlockSpec((1, rows2, tb), lambda bi, h, qi, pt: (h, 0, 0))
    seq_spec = lambda c: pl.BlockSpec(
        (1, t_new, c), lambda bi, h, qi, pt: ((bi * N_KV_HEADS + h) * n_q + qi, 0, 0))
    hbm = pl.BlockSpec(memory_space=pl.ANY)
    grid_spec = pltpu.PrefetchScalarGridSpec(
        num_scalar_prefetch=1, grid=(b, N_KV_HEADS, n_q),
        in_specs=[q_spec, kz_spec, vb_spec, bias_spec, bias_spec,
                  _layer_spec((4, HEAD_DIM), layer), _layer_spec((1, V_DIM), layer),
                  seq_spec(D_MODEL), seq_spec(width), seq_spec(width),
                  _const_spec((N_KV_HEADS, 2 * GROUP * t_new, cols)), hbm, hbm],
        out_specs=[q_spec, seq_spec(D_MODEL)],
        scratch_shapes=[pltpu.VMEM((rows2, V_DIM), BF16),
                        pltpu.VMEM((2, rows2, LANES), F32), pltpu.VMEM((2, rows2, LANES), F32),
                        pltpu.VMEM((2, rows2, V_DIM), F32),
                        pltpu.VMEM((2, n_pages, page_rows, V_DIM), F32),
                        pltpu.VMEM((2, n_pages, page_rows, V_DIM), F32),
                        pltpu.SemaphoreType.DMA((2, 2, n_pages)),
                        pltpu.VMEM((N_KV_HEADS, cols, V_DIM), BF16),
                        pltpu.VMEM((N_KV_HEADS, cols, V_DIM), BF16)])
    return pl.pallas_call(
        functools.partial(_attn_kernel, tb=tb, n_pages=n_pages, t_new=t_new, layer=layer,
                          lam_init=lam_init),
        grid_spec=grid_spec,
        out_shape=[jax.ShapeDtypeStruct((b, s, D_MODEL), BF16),
                   jax.ShapeDtypeStruct((n, t_new, D_MODEL), F32)],
        compiler_params=_params(3),
        name="attention",
    )(page_table.reshape(-1), q, kz, vb, bdiag, bsub, lq, sg, qn, kn, vn, sbias,
      cache_k, cache_v)


def _prompt_bias(rel_bias, tb):
    by_dist = _bias_by_distance(rel_bias, 2 * tb - 1).T
    masked = jnp.full((N_HEADS, tb), NEG_INF, F32)

    def first_row(delta):
        if delta == 0:
            left = jnp.concatenate([by_dist[:, 0:1], masked[:, 1:]], axis=1)
        else:
            left = by_dist[:, delta:delta - tb:-1]
        right = by_dist[:, delta + 1:delta + tb][:, ::-1]
        return jnp.concatenate([left, masked[:, 0:1], right], axis=1)

    rows = jnp.stack([first_row(0), first_row(tb)], axis=1)
    tile = jax.ShapeDtypeStruct((N_HEADS, tb, tb), F32)
    tile_spec = pl.BlockSpec((1, tb, tb), lambda h: (h, 0, 0))
    bdiag, bsub = pl.pallas_call(
        functools.partial(_toeplitz_kernel, tb=tb), grid=(N_HEADS,),
        in_specs=[pl.BlockSpec((1, 2, 2 * tb), lambda h: (h, 0, 0))],
        out_specs=[tile_spec, tile_spec], out_shape=[tile, tile],
        compiler_params=_params(1), name="bias_tiles",
    )(rows)
    shape = (N_KV_HEADS, GROUP * tb, tb)
    return bdiag.reshape(shape), bsub.reshape(shape)


def _toeplitz_kernel(v_ref, diag_ref, sub_ref, *, tb):
    for t, o_ref in enumerate((diag_ref, sub_ref)):
        x = jnp.broadcast_to(v_ref[0, t:t + 1, :], (tb, 2 * tb))
        o_ref[0] = pltpu.roll(x, 0, 1, stride=1, stride_axis=0)[:, :tb]


def _bias_by_distance(rel_bias, max_dist):
    tbl = (rel_bias - rel_bias[N_BUCKETS - 1:N_BUCKETS, :]) * LOG2E
    return tbl[_rel_buckets(max_dist), :]


def _sample_bias(rel_bias, n_pages, t_new):
    past = n_pages * PAGE_SIZE
    cols = past + PAGE_SIZE
    by_dist = _bias_by_distance(rel_bias, past + t_new - 1).T
    rows = []
    for qi in range(t_new):
        seen = by_dist[:, :past + qi + 1][:, ::-1]
        hidden = jnp.full((N_HEADS, cols - (past + qi + 1)), NEG_INF, F32)
        rows.append(jnp.concatenate([seen, hidden], axis=1))
    tile = jnp.stack(rows, axis=1)
    tile = tile.reshape(N_KV_HEADS, GROUP * t_new, cols)
    return jnp.concatenate([tile, tile], axis=1)


def _mixout_kernel(a_ref, cy_ref, sga_ref, sgc_ref, x_ref, wao_ref, wco_ref, wo_ref, o_ref):
    a = jnp.dot(a_ref[...].astype(BF16), wao_ref[...], preferred_element_type=F32)
    c = jnp.dot(cy_ref[...], wco_ref[...], preferred_element_type=F32)
    merged = sga_ref[...].astype(F32) * a + sgc_ref[...].astype(F32) * c
    o_ref[...] = x_ref[...] + jnp.dot(merged.astype(BF16), wo_ref[...],
                                      preferred_element_type=F32)


def _mixout(attn, cy, sga, sgc, x, wao, wco, wo, *, layer):
    t = x.shape[0]
    tm = min(ROW_TILE, t)
    row = pl.BlockSpec((tm, D_MODEL), lambda i: (i, 0))
    w = _layer_spec((D_MODEL, D_MODEL), layer)
    return pl.pallas_call(
        _mixout_kernel, grid=(t // tm,),
        in_specs=[row, row, row, row, row, w, w, w], out_specs=row,
        out_shape=jax.ShapeDtypeStruct((t, D_MODEL), F32),
        compiler_params=_params(1), name="mixout",
    )(attn, cy, sga, sgc, x, wao, wco, wo)


def _ffn_kernel(x_ref, g_ref, wgu_ref, wd_ref, o_ref):
    x = x_ref[...]
    ms = jnp.mean(x * x, axis=-1, keepdims=True)
    h = (x * lax.rsqrt(ms + EPS) * g_ref[...]).astype(BF16)
    y = None
    for a, b in FFN_CHUNKS:
        gate = jnp.dot(h, wgu_ref[:, a:b], preferred_element_type=F32)
        up = jnp.dot(h, wgu_ref[:, FFN_DIM + a:FFN_DIM + b], preferred_element_type=F32)
        act = (gate * jax.nn.sigmoid(gate) * up).astype(BF16)
        part = jnp.dot(act, wd_ref[a:b, :], preferred_element_type=F32)
        y = part if y is None else y + part
    o_ref[...] = x + y


def _ffn(x, g, wgu, wd, *, layer):
    t = x.shape[0]
    tm = min(ROW_TILE, t)
    row = pl.BlockSpec((tm, D_MODEL), lambda i: (i, 0))
    return pl.pallas_call(
        _ffn_kernel, grid=(t // tm,),
        in_specs=[row, _layer_spec((1, D_MODEL), layer),
                  _layer_spec((D_MODEL, 2 * FFN_DIM), layer),
                  _layer_spec((FFN_DIM, D_MODEL), layer)],
        out_specs=row, out_shape=jax.ShapeDtypeStruct((t, D_MODEL), F32),
        compiler_params=_params(1), name="ffn",
    )(x, g, wgu, wd)


def kernel(x_prompt, x_sample, cache_k, cache_v, state_conv, page_table, rel_bias, mix_norm_g, w_in, q_norm_g, k_norm_g, lambda_qk, subln_g, conv_w, w_attn_out, w_conv_out, w_out, ffn_norm_g, w_gate_up, w_down):
    batch, seq, _ = x_prompt.shape
    n_dec, t_new, _ = x_sample.shape
    depth = w_in.shape[0]
    n_pool = cache_k.shape[1]
    n_pages = page_table.shape[1]
    width = N_KV_HEADS * V_DIM

    w_in_b, w_ao_b, w_co_b, w_o_b, w_gu_b, w_d_b = (
        w.astype(BF16) for w in (w_in, w_attn_out, w_conv_out, w_out, w_gate_up, w_down))
    blk = np.arange(D_MODEL) // HEAD_DIM
    bd = jnp.asarray(blk[:, None] == blk[None, :], BF16)
    cache_k4 = cache_k.reshape(depth, n_pool, PAGE_SIZE * N_KV_HEADS, V_DIM)
    cache_v4 = cache_v.reshape(depth, n_pool, PAGE_SIZE * N_KV_HEADS, V_DIM)

    tb = min(ATTN_TILE, seq)
    bdiag, bsub = _prompt_bias(rel_bias, tb)
    sbias = _sample_bias(rel_bias, n_pages, t_new)

    first = jnp.zeros((depth, n_dec, t_new - 1, D_MODEL), F32)
    s1 = jnp.concatenate([state_conv[:, :, 1:2], first], axis=2)
    s2 = jnp.concatenate([state_conv, first[:, :, 1:]], axis=2)
    s1 = s1.reshape(depth, n_dec * t_new, D_MODEL)
    s2 = s2.reshape(depth, n_dec * t_new, D_MODEL)

    xp = x_prompt.reshape(batch * seq, D_MODEL)
    xs = x_sample.reshape(n_dec * t_new, D_MODEL)
    tiles_per_seq = max(seq // min(ROW_TILE, batch * seq), 1)
    conv_p, k_s, v_s, conv_s = [], [], [], []
    new_k = jnp.zeros((depth * batch * seq * N_KV_HEADS, V_DIM), F32)
    new_v = jnp.zeros((depth * batch * seq * N_KV_HEADS, V_DIM), F32)
    g_mix = mix_norm_g.reshape(depth, 1, D_MODEL)
    g_ffn = ffn_norm_g.reshape(depth, 1, D_MODEL)
    qg = jnp.tile(q_norm_g, (1, D_MODEL // HEAD_DIM)).reshape(depth, 1, D_MODEL)
    kg = jnp.tile(k_norm_g, (1, width // HEAD_DIM)).reshape(depth, 1, width)
    sg = subln_g.reshape(depth, 1, V_DIM)
    lq = lambda_qk
    for l in range(depth):
        lam_init = 0.8 - 0.6 * math.exp(-0.3 * l)

        q, kz, vb, new_k, new_v, cy, sga, sgc, cu = _inproj(
            xp, g_mix, w_in_b, bd, qg, kg, conv_w, seq_len=seq,
            new_kv=(new_k, new_v), layer=l)
        qn, k, v, cyn, sgan, sgcn, cun = _inproj(xs, g_mix, w_in_b, bd, qg, kg, conv_w,
                                                 seq_len=t_new, state=(s1, s2), layer=l)
        attn, attn_n = _attention(
            page_table, q.reshape(batch, seq, D_MODEL), kz.reshape(batch, seq, D_MODEL),
            vb.reshape(batch, seq, width), bdiag, bsub, lq, sg,
            qn.reshape(n_dec, t_new, D_MODEL), k.reshape(n_dec, t_new, width),
            v.reshape(n_dec, t_new, width), sbias, cache_k4, cache_v4,
            layer=l, lam_init=lam_init)

        xp = _mixout(attn.reshape(batch * seq, D_MODEL), cy, sga, sgc, xp,
                     w_ao_b, w_co_b, w_o_b, layer=l)
        xp = _ffn(xp, g_ffn, w_gu_b, w_d_b, layer=l)
        tails = cu.reshape(batch, tiles_per_seq, SUBLANES, D_MODEL)
        conv_p.append(tails[:, -1, SUBLANES - (CONV_K - 1):, :])

        xs = _mixout(attn_n.reshape(n_dec * t_new, D_MODEL), cyn, sgan, sgcn, xs,
                     w_ao_b, w_co_b, w_o_b, layer=l)
        xs = _ffn(xs, g_ffn, w_gu_b, w_d_b, layer=l)
        k_s.append(k.reshape(n_dec, t_new, N_KV_HEADS, V_DIM))
        v_s.append(v.reshape(n_dec, t_new, N_KV_HEADS, V_DIM))
        conv_s.append(cun.reshape(n_dec, t_new, D_MODEL)[:, t_new - (CONV_K - 1):, :])

    return (xp.reshape(batch, seq, D_MODEL), xs.reshape(n_dec, t_new, D_MODEL),
            new_k.reshape(depth, batch, seq, N_KV_HEADS, V_DIM),
            new_v.reshape(depth, batch, seq, N_KV_HEADS, V_DIM), jnp.stack(conv_p),
            jnp.stack(k_s), jnp.stack(v_s), jnp.stack(conv_s))
```

```python
import functools
import math

import jax
import jax.numpy as jnp
import numpy as np
from jax import lax
from jax.experimental import pallas as pl
from jax.experimental.pallas import tpu as pltpu

F32 = jnp.float32
BF16 = jnp.bfloat16

D_MODEL = 1024
N_HEADS = 8
N_KV_HEADS = 4
GROUP = N_HEADS // N_KV_HEADS
HEAD_DIM = 64
V_DIM = 2 * HEAD_DIM
ATTN_SCALE = HEAD_DIM ** -0.5
CONV_K = 3
FFN_DIM = 2816
N_BUCKETS = 32
MAX_EXACT = N_BUCKETS // 2
REL_MAX_DIST = 128
EPS = 1e-6
NEG_INF = -1e30
PAGE_SIZE = 128

OFF_Q, OFF_K, OFF_V, OFF_GB, OFF_GC, OFF_U, OFF_GA, OFF_GCV, IN_COLS = (
    0, 1024, 1536, 2048, 3072, 4096, 5120, 6144, 7168)

VMEM_LIMIT_BYTES = 56 * 1024 * 1024
SUBLANES = 8
LANES = 128
LOG2E = math.log2(math.e)
ROW_TILE = 512
ATTN_TILE = 512
FFN_CHUNKS = ((0, 1536), (1536, 2816))


def _params(n_axes, vmem=VMEM_LIMIT_BYTES):
    return pltpu.CompilerParams(
        dimension_semantics=("arbitrary",) * n_axes, vmem_limit_bytes=vmem)


def _const_spec(shape):
    zeros = (0,) * len(shape)
    return pl.BlockSpec(shape, lambda *_: zeros, pipeline_mode=pl.Buffered(1))


def _layer_spec(shape, layer):
    zeros = (0,) * len(shape)
    return pl.BlockSpec((None,) + tuple(shape), lambda *_: (layer,) + zeros,
                        pipeline_mode=pl.Buffered(1))


def _rel_buckets(max_dist):
    n = np.arange(max_dist + 1)
    nf = np.maximum(n, 1).astype(np.float32)
    large = MAX_EXACT + (np.log(nf / np.float32(MAX_EXACT)) / np.float32(math.log(REL_MAX_DIST / MAX_EXACT))
                         * np.float32(N_BUCKETS - MAX_EXACT)).astype(np.int32)
    large = np.minimum(large, N_BUCKETS - 1)
    return np.where(n < MAX_EXACT, n, large).astype(np.int32)


def _inproj_kernel(*refs, tm, tiles_per_seq, sample):
    if sample:
        (x_ref, g_ref, w_ref, bd_ref, qg_ref, kg_ref, cw_ref, s1_ref, s2_ref,
         q_ref, k_ref, v_ref, cy_ref, sga_ref, sgc_ref, cu_ref, ext) = refs
    else:
        (x_ref, g_ref, w_ref, bd_ref, qg_ref, kg_ref, cw_ref, _, _,
         q_ref, kz_ref, vb_ref, knew_ref, vnew_ref, cy_ref, sga_ref, sgc_ref, cu_ref,
         ext) = refs
    i = pl.program_id(0)

    x = x_ref[...]
    ms = jnp.mean(x * x, axis=-1, keepdims=True)
    h = (x * lax.rsqrt(ms + EPS) * g_ref[...]).astype(BF16)

    def proj(a, b):
        return jnp.dot(h, w_ref[:, a:b], preferred_element_type=F32)

    def group_norm(y, bd, gain):
        ss = jnp.dot((y * y).astype(BF16), bd, preferred_element_type=F32)
        return y * lax.rsqrt(ss * (1.0 / HEAD_DIM) + EPS) * gain

    yq = proj(OFF_Q, OFF_K)
    q_ref[...] = (group_norm(yq, bd_ref[...], qg_ref[...]) * (ATTN_SCALE * LOG2E)).astype(q_ref.dtype)
    k = group_norm(proj(OFF_K, OFF_V), bd_ref[0:512, 0:512], kg_ref[...])
    v = proj(OFF_V, OFF_GB)
    if sample:
        k_ref[...] = k
        v_ref[...] = v
    else:
        lane = lax.broadcasted_iota(jnp.int32, (tm, V_DIM), 1)
        for hd in range(N_KV_HEADS):
            kh = k[:, hd * V_DIM:(hd + 1) * V_DIM]
            vh = v[:, hd * V_DIM:(hd + 1) * V_DIM]
            head_rows = pl.ds(hd, tm, stride=N_KV_HEADS)
            knew_ref[head_rows, :] = kh
            vnew_ref[head_rows, :] = vh
            c0 = hd * 2 * V_DIM
            kz_ref[:, c0:c0 + V_DIM] = jnp.where(lane < HEAD_DIM, kh, 0.0).astype(BF16)
            kz_ref[:, c0 + V_DIM:c0 + 2 * V_DIM] = jnp.where(lane >= HEAD_DIM, kh, 0.0).astype(BF16)
            vb_ref[:, hd * V_DIM:(hd + 1) * V_DIM] = vh.astype(BF16)

    @pl.when(i % tiles_per_seq == 0)
    def _():
        ext[0:SUBLANES, :] = jnp.zeros((SUBLANES, D_MODEL), F32)

    cu = proj(OFF_GC, OFF_U) * proj(OFF_U, OFF_GA)
    ext[SUBLANES:SUBLANES + tm, :] = cu
    p1 = ext[SUBLANES - 1:SUBLANES - 1 + tm, :]
    p2 = ext[SUBLANES - 2:SUBLANES - 2 + tm, :]
    if sample:
        rmod = lax.broadcasted_iota(jnp.int32, (tm, D_MODEL), 0) & (SUBLANES - 1)
        p1 = jnp.where(rmod >= 1, p1, s1_ref[...])
        p2 = jnp.where(rmod >= 2, p2, s2_ref[...])
    cw = cw_ref[...]
    conv = cw[0:1, :] * p2 + cw[1:2, :] * p1 + cw[2:3, :] * cu
    cy_ref[...] = (proj(OFF_GB, OFF_GC) * conv).astype(BF16)
    if sample:
        cu_ref[...] = cu
    else:
        cu_ref[...] = cu[tm - SUBLANES:tm, :]
        ext[0:SUBLANES, :] = cu[tm - SUBLANES:tm, :]

    sga_ref[...] = jax.nn.sigmoid(proj(OFF_GA, OFF_GCV)).astype(BF16)
    sgc_ref[...] = jax.nn.sigmoid(proj(OFF_GCV, IN_COLS)).astype(BF16)


def _inproj(x, g, w_in, bd, qg, kg, cw, *, seq_len, state=None, new_kv=None, layer=0):
    t = x.shape[0]
    tm = min(ROW_TILE, t)
    n_tiles = t // tm
    sample = state is not None
    tiles_per_seq = max(seq_len // tm, 1)
    row = lambda c: pl.BlockSpec((tm, c), lambda i: (i, 0))
    in_specs = [row(D_MODEL), _layer_spec((1, D_MODEL), layer),
                _layer_spec((D_MODEL, IN_COLS), layer), _const_spec((D_MODEL, D_MODEL)),
                _layer_spec((1, D_MODEL), layer), _layer_spec((1, 512), layer),
                _layer_spec((CONV_K, D_MODEL), layer)]
    args = [x, g, w_in, bd, qg, kg, cw]
    act = lambda dt: jax.ShapeDtypeStruct((t, D_MODEL), dt)
    if sample:
        in_specs += [pl.BlockSpec((None, tm, D_MODEL), lambda i: (layer, i, 0))] * 2
        args += list(state)
        out_shape = [act(F32), jax.ShapeDtypeStruct((t, 512), F32),
                     jax.ShapeDtypeStruct((t, 512), F32), act(BF16), act(BF16), act(BF16),
                     act(F32)]
        out_specs = [row(D_MODEL), row(512), row(512), row(D_MODEL), row(D_MODEL),
                     row(D_MODEL), row(D_MODEL)]
        aliases = {}
    else:
        in_specs += [pl.BlockSpec(memory_space=pl.ANY)] * 2
        args += list(new_kv)
        new_spec = pl.BlockSpec((tm * N_KV_HEADS, V_DIM), lambda i: (layer * n_tiles + i, 0))
        new_shape = jax.ShapeDtypeStruct(new_kv[0].shape, F32)
        out_shape = [act(BF16), act(BF16), jax.ShapeDtypeStruct((t, 512), BF16),
                     new_shape, new_shape, act(BF16), act(BF16), act(BF16),
                     jax.ShapeDtypeStruct((n_tiles * SUBLANES, D_MODEL), F32)]
        out_specs = [row(D_MODEL), row(D_MODEL), row(512), new_spec, new_spec,
                     row(D_MODEL), row(D_MODEL), row(D_MODEL),
                     pl.BlockSpec((SUBLANES, D_MODEL), lambda i: (i, 0))]
        aliases = {7: 3, 8: 4}
    return pl.pallas_call(
        functools.partial(_inproj_kernel, tm=tm, tiles_per_seq=tiles_per_seq, sample=sample),
        grid=(n_tiles,), in_specs=in_specs, out_specs=out_specs, out_shape=out_shape,
        scratch_shapes=[pltpu.VMEM((tm + SUBLANES, D_MODEL), F32)],
        input_output_aliases=aliases,
        compiler_params=_params(1),
        name="inproj_sample" if sample else "inproj_prompt",
    )(*args)


def _lambda(lq_ref, lam_init):
    lq = lq_ref[...]
    a = jnp.sum(lq[0:1, :] * lq[1:2, :], axis=-1, keepdims=True)
    b = jnp.sum(lq[2:3, :] * lq[3:4, :], axis=-1, keepdims=True)
    return jnp.exp(a) - jnp.exp(b) + lam_init


def _head_norm(o, gain, lam_init):
    ms = jnp.mean(o * o, axis=-1, keepdims=True)
    return o * lax.rsqrt(ms + EPS) * gain * (1.0 - lam_init)


def _attn_kernel(pt_ref, q_ref, kz_ref, vb_ref, bdiag_ref, bsub_ref, lq_ref, sg_ref,
                 qn_ref, kn_ref, vn_ref, sbias_ref, ck_hbm, cv_hbm, o_ref, on_ref,
                 qs_s, m_s, l_s, acc_s, kpg, vpg, sems, kb_s, vb_s,
                 *, tb, n_pages, t_new, layer, lam_init):
    qi = pl.program_id(2)
    n_q = pl.num_programs(2)
    seq = (pl.program_id(0) * N_KV_HEADS + pl.program_id(1)) * n_q + qi
    n_seq = pl.num_programs(0) * N_KV_HEADS * n_q
    slot = seq % 2
    past = n_pages * PAGE_SIZE

    def page_copies(s, buf):
        copies = []
        for j in range(n_pages):
            page = pt_ref[s * n_pages + j]
            copies.append(pltpu.make_async_copy(
                ck_hbm.at[layer, page], kpg.at[buf, j], sems.at[buf, 0, j]))
            copies.append(pltpu.make_async_copy(
                cv_hbm.at[layer, page], vpg.at[buf, j], sems.at[buf, 1, j]))
        return copies

    @pl.when(seq == 0)
    def _():
        for c in page_copies(0, 0):
            c.start()

    @pl.when(seq + 1 < n_seq)
    def _():
        for c in page_copies(seq + 1, 1 - slot):
            c.start()

    for c in page_copies(seq, slot):
        c.wait()

    for g in range(GROUP):
        qs_s[g * tb:(g + 1) * tb, :] = q_ref[0, :, g * V_DIM:(g + 1) * V_DIM]
    m_s[...] = jnp.full(m_s.shape, NEG_INF, F32)
    l_s[...] = jnp.zeros(l_s.shape, F32)
    acc_s[...] = jnp.zeros(acc_s.shape, F32)

    def block(start, width, bias, between=None):
        rows = pl.ds(pl.multiple_of(start, tb), width)
        q = qs_s[...]
        vb = vb_ref[0, rows, :]
        for mi in range(2):
            if between is not None:
                between(2 * mi)
            kz = kz_ref[0, rows, mi * V_DIM:(mi + 1) * V_DIM]
            s = lax.dot_general(q, kz, (((1,), (1,)), ((), ())),
                                preferred_element_type=F32)
            if bias is not None:
                s = s + bias
            if between is not None:
                between(2 * mi + 1)
            cols = [s[:, c * LANES:(c + 1) * LANES] for c in range(width // LANES)]
            m_old = m_s[mi]
            m_new = jnp.maximum(m_old, jnp.max(functools.reduce(jnp.maximum, cols),
                                               axis=-1, keepdims=True))
            alpha = jnp.exp2(m_old - m_new)
            ps = [jnp.exp2(c - m_new) for c in cols]
            l_s[mi] = alpha * l_s[mi] + functools.reduce(jnp.add, ps)
            p = jnp.concatenate([x.astype(BF16) for x in ps], axis=1)
            acc_s[mi] = alpha * acc_s[mi] + jnp.dot(p, vb, preferred_element_type=F32)
            m_s[mi] = m_new

    n_far = jnp.maximum(qi - 1, 0)

    def far(j, carry):
        block(j * (2 * tb), 2 * tb, None)
        return carry

    lax.fori_loop(0, n_far // 2, far, 0)

    @pl.when(n_far % 2 == 1)
    def _():
        block((n_far - 1) * tb, tb, None)

    @pl.when(qi >= 1)
    def _():
        block((qi - 1) * tb, tb, bsub_ref[0])

    lam = _lambda(lq_ref, lam_init)

    qf = qn_ref[0]
    lane = lax.broadcasted_iota(jnp.int32, (t_new, V_DIM), 1)
    pad = jnp.zeros((PAGE_SIZE - t_new, V_DIM), F32)
    heads = range(N_KV_HEADS)
    state = {}

    def sample_phase(i):
        if i == 0:
            for h in heads:
                head_rows = pl.ds(h, PAGE_SIZE, stride=N_KV_HEADS)
                for j in range(n_pages):
                    dst = pl.ds(j * PAGE_SIZE, PAGE_SIZE)
                    kb_s[h, dst, :] = kpg[slot, j, head_rows, :].astype(BF16)
                    vb_s[h, dst, :] = vpg[slot, j, head_rows, :].astype(BF16)
                new = pl.ds(past, PAGE_SIZE)
                hcols = slice(h * V_DIM, (h + 1) * V_DIM)
                kb_s[h, new, :] = jnp.concatenate([kn_ref[0][:, hcols], pad],
                                                  axis=0).astype(BF16)
                vb_s[h, new, :] = jnp.concatenate([vn_ref[0][:, hcols], pad],
                                                  axis=0).astype(BF16)
            state["s"] = []
            for h in heads:
                pieces = []
                for mi in range(2):
                    keep = (lane < HEAD_DIM) if mi == 0 else (lane >= HEAD_DIM)
                    for g in range(GROUP):
                        c0 = (h * GROUP + g) * V_DIM
                        pieces.append(jnp.where(keep, qf[:, c0:c0 + V_DIM], 0.0))
                wq = jnp.concatenate(pieces, axis=0).astype(BF16)
                state["s"].append(lax.dot_general(wq, kb_s[h], (((1,), (1,)), ((), ())),
                                                  preferred_element_type=F32))
        elif i == 1:
            state["p"], state["l"] = [], []
            for h in heads:
                s = state["s"][h] + sbias_ref[h]
                p = jnp.exp2(s - jnp.max(s, axis=-1, keepdims=True))
                state["l"].append(jnp.sum(p, axis=-1, keepdims=True))
                state["p"].append(p.astype(BF16))
        elif i == 2:
            state["acc"] = [jnp.dot(state["p"][h], vb_s[h], preferred_element_type=F32)
                            for h in heads]
        else:
            for h in heads:
                acc = state["acc"][h] / state["l"][h]
                for g in range(GROUP):
                    o1 = acc[g * t_new:(g + 1) * t_new, :]
                    o2 = acc[(GROUP + g) * t_new:(GROUP + g + 1) * t_new, :]
                    c0 = (h * GROUP + g) * V_DIM
                    on_ref[0, :, c0:c0 + V_DIM] = _head_norm(o1 - lam * o2, sg_ref[...],
                                                             lam_init)

    block(qi * tb, tb, bdiag_ref[0], between=sample_phase)

    l1 = jnp.sum(l_s[0], axis=-1, keepdims=True)
    l2 = jnp.sum(l_s[1], axis=-1, keepdims=True)
    o = acc_s[0] / l1 - lam * (acc_s[1] / l2)
    o = _head_norm(o, sg_ref[...], lam_init)
    for g in range(GROUP):
        o_ref[0, :, g * V_DIM:(g + 1) * V_DIM] = o[g * tb:(g + 1) * tb, :].astype(o_ref.dtype)


def _attention(page_table, q, kz, vb, bdiag, bsub, lq, sg, qn, kn, vn, sbias, cache_k, cache_v,
               *, layer, lam_init):
    b, s, _ = q.shape
    n, t_new, _ = qn.shape
    tb = min(ATTN_TILE, s)
    n_q = s // tb
    assert n == b * N_KV_HEADS * n_q, "one sample sequence per attention grid step"
    n_pages = page_table.shape[1]
    width = N_KV_HEADS * V_DIM
    cols = (n_pages + 1) * PAGE_SIZE
    rows2 = GROUP * tb
    page_rows = PAGE_SIZE * N_KV_HEADS

    q_spec = pl.BlockSpec((1, tb, GROUP * V_DIM), lambda bi, h, qi, pt: (bi, qi, h))
    kz_spec = pl.BlockSpec((1, s, 2 * V_DIM), lambda bi, h, qi, pt: (bi, 0, h))
    vb_spec = pl.BlockSpec((1, s, V_DIM), lambda bi, h, qi, pt: (bi, 0, h))
    bias_spec = pl.BlockSpec((1, rows2, tb), lambda bi, h, qi, pt: (h, 0, 0))
    seq_spec = lambda c: pl.BlockSpec(
        (1, t_new, c), lambda bi, h, qi, pt: ((bi * N_KV_HEADS + h) * n_q + qi, 0, 0))
    hbm = pl.BlockSpec(memory_space=pl.ANY)
    grid_spec = pltpu.PrefetchScalarGridSpec(
        num_scalar_prefetch=1, grid=(b, N_KV_HEADS, n_q),
        in_specs=[q_spec, kz_spec, vb_spec, bias_spec, bias_spec,
                  _layer_spec((4, HEAD_DIM), layer), _layer_spec((1, V_DIM), layer),
                  seq_spec(D_MODEL), seq_spec(width), seq_spec(width),
                  _const_spec((N_KV_HEADS, 2 * GROUP * t_new, cols)), hbm, hbm],
        out_specs=[q_spec, seq_spec(D_MODEL)],
        scratch_shapes=[pltpu.VMEM((rows2, V_DIM), BF16),
                        pltpu.VMEM((2, rows2, LANES), F32), pltpu.VMEM((2, rows2, LANES), F32),
                        pltpu.VMEM((2, rows2, V_DIM), F32),
                        pltpu.VMEM((2, n_pages, page_rows, V_DIM), F32),
                        pltpu.VMEM((2, n_pages, page_rows, V_DIM), F32),
                        pltpu.SemaphoreType.DMA((2, 2, n_pages)),
                        pltpu.VMEM((N_KV_HEADS, cols, V_DIM), BF16),
                        pltpu.VMEM((N_KV_HEADS, cols, V_DIM), BF16)])
    return pl.pallas_call(
        functools.partial(_attn_kernel, tb=tb, n_pages=n_pages, t_new=t_new, layer=layer,
                          lam_init=lam_init),
        grid_spec=grid_spec,
        out_shape=[jax.ShapeDtypeStruct((b, s, D_MODEL), BF16),
                   jax.ShapeDtypeStruct((n, t_new, D_MODEL), F32)],
        compiler_params=_params(3),
        name="attention",
    )(page_table.reshape(-1), q, kz, vb, bdiag, bsub, lq, sg, qn, kn, vn, sbias,
      cache_k, cache_v)


def _prompt_bias(rel_bias, tb):
    by_dist = _bias_by_distance(rel_bias, 2 * tb - 1).T
    masked = jnp.full((N_HEADS, tb), NEG_INF, F32)

    def first_row(delta):
        if delta == 0:
            left = jnp.concatenate([by_dist[:, 0:1], masked[:, 1:]], axis=1)
        else:
            left = by_dist[:, delta:delta - tb:-1]
        right = by_dist[:, delta + 1:delta + tb][:, ::-1]
        return jnp.concatenate([left, masked[:, 0:1], right], axis=1)

    rows = jnp.stack([first_row(0), first_row(tb)], axis=1)
    tile = jax.ShapeDtypeStruct((N_HEADS, tb, tb), F32)
    tile_spec = pl.BlockSpec((1, tb, tb), lambda h: (h, 0, 0))
    bdiag, bsub = pl.pallas_call(
        functools.partial(_toeplitz_kernel, tb=tb), grid=(N_HEADS,),
        in_specs=[pl.BlockSpec((1, 2, 2 * tb), lambda h: (h, 0, 0))],
        out_specs=[tile_spec, tile_spec], out_shape=[tile, tile],
        compiler_params=_params(1), name="bias_tiles",
    )(rows)
    shape = (N_KV_HEADS, GROUP * tb, tb)
    return bdiag.reshape(shape), bsub.reshape(shape)


def _toeplitz_kernel(v_ref, diag_ref, sub_ref, *, tb):
    for t, o_ref in enumerate((diag_ref, sub_ref)):
        x = jnp.broadcast_to(v_ref[0, t:t + 1, :], (tb, 2 * tb))
        o_ref[0] = pltpu.roll(x, 0, 1, stride=1, stride_axis=0)[:, :tb]


def _bias_by_distance(rel_bias, max_dist):
    tbl = (rel_bias - rel_bias[N_BUCKETS - 1:N_BUCKETS, :]) * LOG2E
    return tbl[_rel_buckets(max_dist), :]


def _sample_bias(rel_bias, n_pages, t_new):
    past = n_pages * PAGE_SIZE
    cols = past + PAGE_SIZE
    by_dist = _bias_by_distance(rel_bias, past + t_new - 1).T
    rows = []
    for qi in range(t_new):
        seen = by_dist[:, :past + qi + 1][:, ::-1]
        hidden = jnp.full((N_HEADS, cols - (past + qi + 1)), NEG_INF, F32)
        rows.append(jnp.concatenate([seen, hidden], axis=1))
    tile = jnp.stack(rows, axis=1)
    tile = tile.reshape(N_KV_HEADS, GROUP * t_new, cols)
    return jnp.concatenate([tile, tile], axis=1)


def _mixout_kernel(a_ref, cy_ref, sga_ref, sgc_ref, x_ref, wao_ref, wco_ref, wo_ref, o_ref):
    a = jnp.dot(a_ref[...].astype(BF16), wao_ref[...], preferred_element_type=F32)
    c = jnp.dot(cy_ref[...], wco_ref[...], preferred_element_type=F32)
    merged = sga_ref[...].astype(F32) * a + sgc_ref[...].astype(F32) * c
    o_ref[...] = x_ref[...] + jnp.dot(merged.astype(BF16), wo_ref[...],
                                      preferred_element_type=F32)


def _mixout(attn, cy, sga, sgc, x, wao, wco, wo, *, layer):
    t = x.shape[0]
    tm = min(ROW_TILE, t)
    row = pl.BlockSpec((tm, D_MODEL), lambda i: (i, 0))
    w = _layer_spec((D_MODEL, D_MODEL), layer)
    return pl.pallas_call(
        _mixout_kernel, grid=(t // tm,),
        in_specs=[row, row, row, row, row, w, w, w], out_specs=row,
        out_shape=jax.ShapeDtypeStruct((t, D_MODEL), F32),
        compiler_params=_params(1), name="mixout",
    )(attn, cy, sga, sgc, x, wao, wco, wo)


def _ffn_kernel(x_ref, g_ref, wgu_ref, wd_ref, o_ref):
    x = x_ref[...]
    ms = jnp.mean(x * x, axis=-1, keepdims=True)
    h = (x * lax.rsqrt(ms + EPS) * g_ref[...]).astype(BF16)
    y = None
    for a, b in FFN_CHUNKS:
        gate = jnp.dot(h, wgu_ref[:, a:b], preferred_element_type=F32)
        up = jnp.dot(h, wgu_ref[:, FFN_DIM + a:FFN_DIM + b], preferred_element_type=F32)
        act = (gate * jax.nn.sigmoid(gate) * up).astype(BF16)
        part = jnp.dot(act, wd_ref[a:b, :], preferred_element_type=F32)
        y = part if y is None else y + part
    o_ref[...] = x + y


def _ffn(x, g, wgu, wd, *, layer):
    t = x.shape[0]
    tm = min(ROW_TILE, t)
    row = pl.BlockSpec((tm, D_MODEL), lambda i: (i, 0))
    return pl.pallas_call(
        _ffn_kernel, grid=(t // tm,),
        in_specs=[row, _layer_spec((1, D_MODEL), layer),
                  _layer_spec((D_MODEL, 2 * FFN_DIM), layer),
                  _layer_spec((FFN_DIM, D_MODEL), layer)],
        out_specs=row, out_shape=jax.ShapeDtypeStruct((t, D_MODEL), F32),
        compiler_params=_params(1), name="ffn",
    )(x, g, wgu, wd)


def kernel(x_prompt, x_sample, cache_k, cache_v, state_conv, page_table, rel_bias, mix_norm_g, w_in, q_norm_g, k_norm_g, lambda_qk, subln_g, conv_w, w_attn_out, w_conv_out, w_out, ffn_norm_g, w_gate_up, w_down):
    batch, seq, _ = x_prompt.shape
    n_dec, t_new, _ = x_sample.shape
    depth = w_in.shape[0]
    n_pool = cache_k.shape[1]
    n_pages = page_table.shape[1]
    width = N_KV_HEADS * V_DIM

    w_in_b, w_ao_b, w_co_b, w_o_b, w_gu_b, w_d_b = (
        w.astype(BF16) for w in (w_in, w_attn_out, w_conv_out, w_out, w_gate_up, w_down))
    blk = np.arange(D_MODEL) // HEAD_DIM
    bd = jnp.asarray(blk[:, None] == blk[None, :], BF16)
    cache_k4 = cache_k.reshape(depth, n_pool, PAGE_SIZE * N_KV_HEADS, V_DIM)
    cache_v4 = cache_v.reshape(depth, n_pool, PAGE_SIZE * N_KV_HEADS, V_DIM)

    tb = min(ATTN_TILE, seq)
    bdiag, bsub = _prompt_bias(rel_bias, tb)
    sbias = _sample_bias(rel_bias, n_pages, t_new)

    first = jnp.zeros((depth, n_dec, t_new - 1, D_MODEL), F32)
    s1 = jnp.concatenate([state_conv[:, :, 1:2], first], axis=2)
    s2 = jnp.concatenate([state_conv, first[:, :, 1:]], axis=2)
    s1 = s1.reshape(depth, n_dec * t_new, D_MODEL)
    s2 = s2.reshape(depth, n_dec * t_new, D_MODEL)

    xp = x_prompt.reshape(batch * seq, D_MODEL)
    xs = x_sample.reshape(n_dec * t_new, D_MODEL)
    tiles_per_seq = max(seq // min(ROW_TILE, batch * seq), 1)
    conv_p, k_s, v_s, conv_s = [], [], [], []
    new_k = jnp.zeros((depth * batch * seq * N_KV_HEADS, V_DIM), F32)
    new_v = jnp.zeros((depth * batch * seq * N_KV_HEADS, V_DIM), F32)
    g_mix = mix_norm_g.reshape(depth, 1, D_MODEL)
    g_ffn = ffn_norm_g.reshape(depth, 1, D_MODEL)
    qg = jnp.tile(q_norm_g, (1, D_MODEL // HEAD_DIM)).reshape(depth, 1, D_MODEL)
    kg = jnp.tile(k_norm_g, (1, width // HEAD_DIM)).reshape(depth, 1, width)
    sg = subln_g.reshape(depth, 1, V_DIM)
    lq = lambda_qk
    for l in range(depth):
        lam_init = 0.8 - 0.6 * math.exp(-0.3 * l)

        q, kz, vb, new_k, new_v, cy, sga, sgc, cu = _inproj(
            xp, g_mix, w_in_b, bd, qg, kg, conv_w, seq_len=seq,
            new_kv=(new_k, new_v), layer=l)
        qn, k, v, cyn, sgan, sgcn, cun = _inproj(xs, g_mix, w_in_b, bd, qg, kg, conv_w,
                                                 seq_len=t_new, state=(s1, s2), layer=l)
        attn, attn_n = _attention(
            page_table, q.reshape(batch, seq, D_MODEL), kz.reshape(batch, seq, D_MODEL),
            vb.reshape(batch, seq, width), bdiag, bsub, lq, sg,
            qn.reshape(n_dec, t_new, D_MODEL), k.reshape(n_dec, t_new, width),
            v.reshape(n_dec, t_new, width), sbias, cache_k4, cache_v4,
            layer=l, lam_init=lam_init)

        xp = _mixout(attn.reshape(batch * seq, D_MODEL), cy, sga, sgc, xp,
                     w_ao_b, w_co_b, w_o_b, layer=l)
        xp = _ffn(xp, g_ffn, w_gu_b, w_d_b, layer=l)
        tails = cu.reshape(batch, tiles_per_seq, SUBLANES, D_MODEL)
        conv_p.append(tails[:, -1, SUBLANES - (CONV_K - 1):, :])

        xs = _mixout(attn_n.reshape(n_dec * t_new, D_MODEL), cyn, sgan, sgcn, xs,
                     w_ao_b, w_co_b, w_o_b, layer=l)
        xs = _ffn(xs, g_ffn, w_gu_b, w_d_b, layer=l)
        k_s.append(k.reshape(n_dec, t_new, N_KV_HEADS, V_DIM))
        v_s.append(v.reshape(n_dec, t_new, N_KV_HEADS, V_DIM))
        conv_s.append(cun.reshape(n_dec, t_new, D_MODEL)[:, t_new - (CONV_K - 1):, :])

    return (xp.reshape(batch, seq, D_MODEL), xs.reshape(n_dec, t_new, D_MODEL),
            new_k.reshape(depth, batch, seq, N_KV_HEADS, V_DIM),
            new_v.reshape(depth, batch, seq, N_KV_HEADS, V_DIM), jnp.stack(conv_p),
            jnp.stack(k_s), jnp.stack(v_s), jnp.stack(conv_s))
```
